```python
import math
import jax, jax.numpy as jnp
from jax import lax
import numpy as np

D_MODEL = 2048
BATCH = 4
SEQ = 2048
DEPTH = 4

N_EVEN = (DEPTH + 1) // 2
N_ODD = DEPTH // 2
NORM_EPS = 1e-6
NEG_INF = -1e30
FORCE_SCORE = 1e30
TINY = 1e-30

D_A = D_MODEL
A_DK = 128
A_HEADS = D_A // 128
A_DV = D_A // A_HEADS
A_QK = A_HEADS * A_DK
A_CHUNK = 64

D_B = D_MODEL
B_HEADS = 16
B_DH = D_B // B_HEADS
B_KV = 4
B_HPG = B_HEADS // B_KV
B_KVW = B_KV * B_DH
CMP_LEN = 32
CMP_STRIDE = 16
SEL_LEN = 64
SEL_TOPK = 16
WIN = 512
WIN_QB = 128
SEL_QB = 32

D_RNN = 2560
RG_BLOCKS = 10
RG_BS = D_RNN // RG_BLOCKS
CONV_W = 4
RG_C = 8.0

EVEN_SIZES = (A_QK, A_QK, D_A, D_A, D_B, B_KVW, B_KVW, B_KVW, B_KVW, B_KVW, B_KVW, 3 * B_HEADS, D_B)
EVEN_IN = sum(EVEN_SIZES)
EVEN_SPLIT_AT = tuple(int(v) for v in np.cumsum(EVEN_SIZES)[:-1])
EVEN_MIX = D_A + D_B

kernel_name = "hybrid_hgrn2_nsa_rglru_trunk"


def rms_norm(x, w):
    xf = x.astype(jnp.float32)
    y = xf * lax.rsqrt(jnp.mean(xf * xf, axis=-1, keepdims=True) + NORM_EPS)
    return (y * w.astype(jnp.float32)).astype(x.dtype)


def alibi_slopes(n):
    return jnp.asarray(2.0 ** (-8.0 * np.arange(1, n + 1) / n), dtype=jnp.float32)


def hgrn2(q, fz, v, lb, gain):
    bsz, T, _ = q.shape
    f32 = jnp.float32
    qf = jax.nn.silu(q.astype(f32))
    fz = fz.astype(f32)
    f = lb + (1.0 - lb) * jax.nn.sigmoid(fz)
    logf = jnp.log(jnp.maximum(f, TINY))
    k = (1.0 - lb) * jax.nn.sigmoid(-fz)
    vf = v.astype(f32)
    nc = T // A_CHUNK

    def to_chunks(a, d):
        return a.reshape(bsz, nc, A_CHUNK, A_HEADS, d).transpose(1, 0, 3, 2, 4)

    qc, kc, gc, vc = to_chunks(qf, A_DK), to_chunks(k, A_DK), to_chunks(logf, A_DK), to_chunks(vf, A_DV)
    causal = jnp.tril(jnp.ones((A_CHUNK, A_CHUNK), dtype=bool))[None, None, :, :, None]

    def step(S, inp):
        qt, kt, gt, vt = inp
        b = jnp.cumsum(gt, axis=2)
        o_inter = jnp.einsum('bhtk,bhkv->bhtv', qt * jnp.exp(b), S)
        rel = b[:, :, :, None, :] - b[:, :, None, :, :]
        decay = jnp.exp(jnp.where(causal, rel, NEG_INF))
        att = jnp.einsum('bhtk,bhsk,bhtsk->bhts', qt, kt, decay)
        o = o_inter + jnp.einsum('bhts,bhsv->bhtv', att, vt)
        b_last = b[:, :, -1:, :]
        S = jnp.exp(b_last[:, :, 0, :])[..., None] * S + jnp.einsum('bhsk,bhsv->bhkv', kt * jnp.exp(b_last - b), vt)
        return S, o

    S0 = jnp.zeros((bsz, A_HEADS, A_DK, A_DV), f32)
    _, o = lax.scan(step, S0, (qc, kc, gc, vc))
    o = o.transpose(1, 0, 3, 2, 4).reshape(bsz, T, A_HEADS, A_DV)
    o = o * lax.rsqrt(jnp.mean(o * o, axis=-1, keepdims=True) + NORM_EPS)
    return (o.reshape(bsz, T, D_A) * gain.astype(f32)).astype(q.dtype)


def nsa(q, kc, vc, ks, vs, kw, vw, gate_logits, pe_k, w1_k, w2_k, pe_v, w1_v, w2_v):
    bsz, T, _ = q.shape
    f32 = jnp.float32
    dt = q.dtype
    scale = B_DH ** -0.5
    slopes = alibi_slopes(B_HEADS).reshape(B_KV, B_HPG)

    def heads(a, n):
        return a.reshape(bsz, T, n, B_DH).transpose(0, 2, 1, 3)

    qh = heads(q, B_HEADS).reshape(bsz, B_KV, B_HPG, T, B_DH)
    pos = np.arange(T)

    n_cmp = (T - CMP_LEN) // CMP_STRIDE + 1
    cmp_idx = np.arange(n_cmp)[:, None] * CMP_STRIDE + np.arange(CMP_LEN)[None, :]

    def compress(a, pe, w1, w2):
        blk = heads(a, B_KV)[:, :, cmp_idx] + pe
        hid = jax.nn.silu(blk.reshape(bsz, B_KV, n_cmp, CMP_LEN * B_DH) @ w1)
        return hid @ w2

    Kc = compress(kc, pe_k, w1_k, w2_k)
    Vc = compress(vc, pe_v, w1_v, w2_v)
    dist_c = pos[:, None] - cmp_idx[None, :, -1]
    mask_c = dist_c >= 0
    s = jnp.einsum('bghtd,bgnd->bghtn', qh, Kc).astype(f32) * scale - slopes[..., None, None] * dist_c.astype(np.float32)
    p_cmp = jnp.where(mask_c, jax.nn.softmax(jnp.where(mask_c, s, NEG_INF), axis=-1), 0.0)
    o_cmp = jnp.einsum('bghtn,bgnd->bghtd', p_cmp.astype(dt), Vc)

    n_slc = T // SEL_LEN
    c_start = np.arange(n_cmp) * CMP_STRIDE
    s_start = np.arange(n_slc) * SEL_LEN
    overlap = ((c_start[:, None] <= s_start[None, :] + SEL_LEN - 1) &
               (c_start[:, None] + CMP_LEN - 1 >= s_start[None, :])).astype(np.float32)
    imp = jnp.einsum('bghtn,nj->bgtj', p_cmp, overlap)
    blk = np.arange(n_slc)[None, :]
    cur = (pos // SEL_LEN)[:, None]
    valid = blk <= cur
    forced = (blk == 0) | (blk == cur) | (blk == cur - 1)
    score = jnp.where(forced, FORCE_SCORE, jnp.where(valid, imp, NEG_INF))
    topk = min(SEL_TOPK, n_slc)
    _, idx = lax.top_k(score, topk)

    Ks = heads(ks, B_KV).reshape(bsz, B_KV, n_slc, SEL_LEN, B_DH)
    Vs = heads(vs, B_KV).reshape(bsz, B_KV, n_slc, SEL_LEN, B_DH)
    nqb = T // SEL_QB
    q_blocks = qh.reshape(bsz, B_KV, B_HPG, nqb, SEL_QB, B_DH).transpose(3, 0, 1, 2, 4, 5)
    idx_blocks = idx.reshape(bsz, B_KV, nqb, SEL_QB, topk).transpose(2, 0, 1, 3, 4)
    t_blocks = jnp.arange(T).reshape(nqb, SEL_QB)
    bi = jnp.arange(bsz)[:, None, None, None]
    gi = jnp.arange(B_KV)[None, :, None, None]
    offs = jnp.arange(SEL_LEN)

    def sel_block(args):
        qb, ib, tb = args
        kg = Ks[bi, gi, ib].reshape(bsz, B_KV, SEL_QB, topk * SEL_LEN, B_DH)
        vg = Vs[bi, gi, ib].reshape(bsz, B_KV, SEL_QB, topk * SEL_LEN, B_DH)
        kpos = (ib[..., None] * SEL_LEN + offs).reshape(bsz, B_KV, SEL_QB, topk * SEL_LEN)
        d = (tb[None, None, :, None] - kpos)[:, :, None]
        sb = jnp.einsum('bghqd,bgqsd->bghqs', qb, kg).astype(f32) * scale - slopes[:, :, None, None] * d.astype(f32)
        pb = jax.nn.softmax(jnp.where(d >= 0, sb, NEG_INF), axis=-1).astype(dt)
        return jnp.einsum('bghqs,bgqsd->bghqd', pb, vg)

    o_sel = lax.map(sel_block, (q_blocks, idx_blocks, t_blocks))
    o_sel = o_sel.transpose(1, 2, 3, 0, 4, 5).reshape(bsz, B_KV, B_HPG, T, B_DH)

    n_wb = T // WIN_QB
    span = WIN + WIN_QB
    kpos_w = np.arange(n_wb)[:, None] * WIN_QB - WIN + np.arange(span)[None, :]
    kidx = np.clip(kpos_w, 0, T - 1)
    Kw = heads(kw, B_KV)[:, :, kidx]
    Vw = heads(vw, B_KV)[:, :, kidx]
    qw = qh.reshape(bsz, B_KV, B_HPG, n_wb, WIN_QB, B_DH)
    d_w = pos.reshape(n_wb, WIN_QB)[:, :, None] - kpos_w[:, None, :]
    mask_w = (d_w >= 0) & (d_w < WIN) & (kpos_w[:, None, :] >= 0)
    sw = jnp.einsum('bghnqd,bgnkd->bghnqk', qw, Kw).astype(f32) * scale - slopes[:, :, None, None, None] * d_w.astype(np.float32)
    pw = jax.nn.softmax(jnp.where(mask_w, sw, NEG_INF), axis=-1).astype(dt)
    o_win = jnp.einsum('bghnqk,bgnkd->bghnqd', pw, Vw).reshape(bsz, B_KV, B_HPG, T, B_DH)

    g = jax.nn.sigmoid(gate_logits.astype(f32)).reshape(bsz, T, B_KV, B_HPG, 3).transpose(0, 2, 3, 1, 4)
    o = (g[..., 0:1] * o_cmp.astype(f32) + g[..., 1:2] * o_sel.astype(f32) + g[..., 2:3] * o_win.astype(f32))
    return o.transpose(0, 3, 1, 2, 4).reshape(bsz, T, D_B).astype(dt)


def rglru(xb, conv_w, conv_b, w_a, b_a, w_i, b_i, lam):
    bsz, T, _ = xb.shape
    f32 = jnp.float32
    xp = jnp.pad(xb, ((0, 0), (CONV_W - 1, 0), (0, 0)))
    xc = sum(xp[:, j:j + T] * conv_w[j] for j in range(CONV_W)) + conv_b
    xg = xc.reshape(bsz, T, RG_BLOCKS, RG_BS)
    r = jax.nn.sigmoid(jnp.einsum('btnd,nde->btne', xg, w_a).reshape(bsz, T, D_RNN).astype(f32) + b_a.astype(f32))
    i = jax.nn.sigmoid(jnp.einsum('btnd,nde->btne', xg, w_i).reshape(bsz, T, D_RNN).astype(f32) + b_i.astype(f32))
    log_a = -RG_C * jax.nn.softplus(-lam.astype(f32)) * r
    a = jnp.exp(log_a)
    u = jnp.sqrt(jnp.maximum(-jnp.expm1(2.0 * log_a), 0.0)) * (i * xc.astype(f32))

    def combine(c1, c2):
        a1, b1 = c1
        a2, b2 = c2
        return a1 * a2, a2 * b1 + b2

    _, h = lax.associative_scan(combine, (a, u), axis=1)
    return h.astype(xb.dtype)


def setup_inputs(seed: int = 0) -> dict:
    key = jax.random.key(seed)
    ks = jax.random.split(key, 24)
    nrm = jax.random.normal
    f32 = jnp.float32
    lo, hi = 0.9 ** (1.0 / RG_C), 0.999 ** (1.0 / RG_C)
    u = jax.random.uniform(ks[21], (N_ODD, D_RNN), f32, minval=lo, maxval=hi)
    return {
        'x': nrm(ks[0], (BATCH, SEQ, D_MODEL), f32),
        'norm_w': 1.0 + 0.02 * nrm(ks[1], (DEPTH, D_MODEL), f32),
        'final_norm_w': 1.0 + 0.02 * nrm(ks[2], (D_MODEL,), f32),
        'even_w_in': nrm(ks[3], (N_EVEN, D_MODEL, EVEN_IN), f32) * D_MODEL ** -0.5,
        'even_w_out': nrm(ks[4], (N_EVEN, EVEN_MIX, D_MODEL), f32) * EVEN_MIX ** -0.5,
        'hgrn_lb_logits': 0.5 * nrm(ks[5], (N_EVEN, A_QK), f32),
        'hgrn_norm_w': 1.0 + 0.02 * nrm(ks[6], (N_EVEN, D_A), f32),
        'cmp_pe_k': 0.02 * nrm(ks[7], (N_EVEN, CMP_LEN, B_DH), f32),
        'cmp_w1_k': nrm(ks[8], (N_EVEN, CMP_LEN * B_DH, B_DH), f32) * (CMP_LEN * B_DH) ** -0.5,
        'cmp_w2_k': nrm(ks[9], (N_EVEN, B_DH, B_DH), f32) * B_DH ** -0.5,
        'cmp_pe_v': 0.02 * nrm(ks[10], (N_EVEN, CMP_LEN, B_DH), f32),
        'cmp_w1_v': nrm(ks[11], (N_EVEN, CMP_LEN * B_DH, B_DH), f32) * (CMP_LEN * B_DH) ** -0.5,
        'cmp_w2_v': nrm(ks[12], (N_EVEN, B_DH, B_DH), f32) * B_DH ** -0.5,
        'odd_w_in': nrm(ks[13], (N_ODD, D_MODEL, 2 * D_RNN), f32) * D_MODEL ** -0.5,
        'odd_w_out': nrm(ks[14], (N_ODD, D_RNN, D_MODEL), f32) * D_RNN ** -0.5,
        'rg_conv_w': nrm(ks[15], (N_ODD, CONV_W, D_RNN), f32) * CONV_W ** -0.5,
        'rg_conv_b': 0.01 * nrm(ks[16], (N_ODD, D_RNN), f32),
        'rg_w_a': nrm(ks[17], (N_ODD, RG_BLOCKS, RG_BS, RG_BS), f32) * RG_BS ** -0.5,
        'rg_b_a': 0.01 * nrm(ks[18], (N_ODD, D_RNN), f32),
        'rg_w_i': nrm(ks[19], (N_ODD, RG_BLOCKS, RG_BS, RG_BS), f32) * RG_BS ** -0.5,
        'rg_b_i': 0.01 * nrm(ks[20], (N_ODD, D_RNN), f32),
        'rg_lambda': jnp.log(u) - jnp.log1p(-u),
    }


def reference(x, norm_w, final_norm_w, even_w_in, even_w_out, hgrn_lb_logits, hgrn_norm_w,
              cmp_pe_k, cmp_w1_k, cmp_w2_k, cmp_pe_v, cmp_w1_v, cmp_w2_v,
              odd_w_in, odd_w_out, rg_conv_w, rg_conv_b, rg_w_a, rg_b_a, rg_w_i, rg_b_i, rg_lambda):
    lb_sm = jax.nn.softmax(hgrn_lb_logits.astype(jnp.float32), axis=0)
    lb_all = jnp.cumsum(lb_sm, axis=0) - lb_sm[0]
    for layer in range(DEPTH):
        h = rms_norm(x, norm_w[layer])
        if layer % 2 == 0:
            e = layer // 2
            (a_q, a_f, a_i, a_g, b_q, b_kc, b_vc, b_ks, b_vs, b_kw, b_vw, b_gate, b_g) = jnp.split(
                h @ even_w_in[e], EVEN_SPLIT_AT, axis=-1)
            ya = hgrn2(a_q, a_f, a_i, lb_all[e], hgrn_norm_w[e]) * jax.nn.silu(a_g)
            yb = nsa(b_q, b_kc, b_vc, b_ks, b_vs, b_kw, b_vw, b_gate,
                     cmp_pe_k[e], cmp_w1_k[e], cmp_w2_k[e], cmp_pe_v[e], cmp_w1_v[e], cmp_w2_v[e]) * jax.nn.silu(b_g)
            y = jnp.concatenate([ya, yb], axis=-1) @ even_w_out[e]
        else:
            o = layer // 2
            xb, g = jnp.split(h @ odd_w_in[o], [D_RNN], axis=-1)
            hr = rglru(xb, rg_conv_w[o], rg_conv_b[o], rg_w_a[o], rg_b_a[o], rg_w_i[o], rg_b_i[o], rg_lambda[o])
            y = (hr * jax.nn.silu(g)) @ odd_w_out[o]
        x = x + y.astype(x.dtype)
    return rms_norm(x, final_norm_w)
```

```python
import functools
import math

import jax
import jax.numpy as jnp
import numpy as np
from jax import lax
from jax.experimental import pallas as pl
from jax.experimental.pallas import tpu as pltpu

F32 = jnp.float32
BF16 = jnp.bfloat16

NORM_EPS = 1e-6
NEG_INF = -1e30
FORCE_SCORE = 1e30
TINY = 1e-30

A_HEADS = 16
A_DK = 128
A_DV = 128
HG_CHUNK = 64
HG_SUB = 16

B_HEADS = 16
B_DH = 128
B_KV = 4
B_HPG = B_HEADS // B_KV
CMP_LEN = 32
CMP_STRIDE = 16
SEL_LEN = 64
SEL_TOPK = 16
WIN = 512
NSA_TQ = 128
NSA_KB = 512

RG_BLOCKS = 10
RG_BS = 256
CONV_W = 4
RG_C = 8.0
RG_TB = 256

VMEM_LIMIT = 48 * 1024 * 1024

_NT = (((1,), (1,)), ((), ()))
_TN = (((0,), (0,)), ((), ()))


def _silu(x):
    return x * jax.nn.sigmoid(x)


def _params(*sem):
    return pltpu.CompilerParams(dimension_semantics=sem, vmem_limit_bytes=VMEM_LIMIT)


def _norm_matmul_kernel(x_ref, nw_ref, w_ref, o_ref, h_ref):
    @pl.when(pl.program_id(1) == 0)
    def _():
        x = x_ref[...]
        ms = jnp.mean(x * x, axis=-1, keepdims=True)
        h_ref[...] = (x * lax.rsqrt(ms + NORM_EPS) * nw_ref[...]).astype(BF16)

    o_ref[...] = jnp.dot(h_ref[...], w_ref[...], preferred_element_type=F32)


def norm_matmul(x, nw, w, tm=1024, tn=512):
    m, d = x.shape
    n = w.shape[1]
    return pl.pallas_call(
        _norm_matmul_kernel,
        grid=(m // tm, n // tn),
        in_specs=[
            pl.BlockSpec((tm, d), lambda i, j: (i, 0)),
            pl.BlockSpec((1, d), lambda i, j: (0, 0)),
            pl.BlockSpec((d, tn), lambda i, j: (0, j)),
        ],
        out_specs=pl.BlockSpec((tm, tn), lambda i, j: (i, j)),
        out_shape=jax.ShapeDtypeStruct((m, n), F32),
        scratch_shapes=[pltpu.VMEM((tm, d), BF16)],
        compiler_params=_params("parallel", "arbitrary"),
        name="norm_matmul",
    )(x, nw.reshape(1, d), w)


def _proj_residual_kernel(*refs, n_in):
    a_refs, w_refs = refs[:n_in], refs[n_in:2 * n_in]
    x_ref, o_ref = refs[2 * n_in], refs[2 * n_in + 1]
    y = jnp.dot(a_refs[0][...], w_refs[0][...], preferred_element_type=F32)
    for a_ref, w_ref in zip(a_refs[1:], w_refs[1:]):
        y = y + jnp.dot(a_ref[...], w_ref[...], preferred_element_type=F32)
    o_ref[...] = x_ref[...] + y


def proj_residual(acts, ws, x, tm=1024, tn=512):
    m, n = x.shape
    n_in = len(acts)
    in_specs = [pl.BlockSpec((tm, a.shape[1]), lambda i, j: (i, 0)) for a in acts]
    in_specs += [pl.BlockSpec((w.shape[0], tn), lambda i, j: (0, j)) for w in ws]
    in_specs += [pl.BlockSpec((tm, tn), lambda i, j: (i, j))]
    return pl.pallas_call(
        functools.partial(_proj_residual_kernel, n_in=n_in),
        grid=(m // tm, n // tn),
        in_specs=in_specs,
        out_specs=pl.BlockSpec((tm, tn), lambda i, j: (i, j)),
        out_shape=jax.ShapeDtypeStruct((m, n), F32),
        compiler_params=_params("parallel", "arbitrary"),
        name="proj_residual",
    )(*acts, *ws, x)


def _rmsnorm_kernel(x_ref, w_ref, o_ref):
    x = x_ref[...]
    ms = jnp.mean(x * x, axis=-1, keepdims=True)
    o_ref[...] = x * lax.rsqrt(ms + NORM_EPS) * w_ref[...]


def rmsnorm(x, w, tm=512):
    m, d = x.shape
    return pl.pallas_call(
        _rmsnorm_kernel,
        grid=(m // tm,),
        in_specs=[pl.BlockSpec((tm, d), lambda i: (i, 0)), pl.BlockSpec((1, d), lambda i: (0, 0))],
        out_specs=pl.BlockSpec((tm, d), lambda i: (i, 0)),
        out_shape=jax.ShapeDtypeStruct((m, d), F32),
        compiler_params=_params("parallel"),
        name="final_rmsnorm",
    )(x, w.reshape(1, d))


def _split3(x):
    hi = x.astype(BF16)
    r1 = x - hi.astype(F32)
    mid = r1.astype(BF16)
    lo = (r1 - mid.astype(F32)).astype(BF16)
    return hi, mid, lo


def _hgrn2_kernel(lbl_ref, gain_ref, q_ref, f_ref, v_ref, g_ref, o_ref, st_ref, b_ref, k_ref, *, layer, seq):
    c_len, sub = HG_CHUNK, HG_SUB
    lg = lbl_ref[...]
    ex = jnp.exp(lg - jnp.max(lg, axis=0, keepdims=True))
    sm = ex / jnp.sum(ex, axis=0, keepdims=True)
    lb = sm[0:1]
    for j in range(1, layer + 1):
        lb = lb + sm[j:j + 1]
    lb = lb - sm[0:1]
    one_m_lb = 1.0 - lb
    gain = gain_ref[...]

    st_ref[...] = jnp.zeros_like(st_ref)
    tri = jnp.where(lax.broadcasted_iota(jnp.int32, (c_len, c_len), 0)
                    >= lax.broadcasted_iota(jnp.int32, (c_len, c_len), 1), 1.0, 0.0).astype(BF16)
    ones_kk = jnp.ones((A_DK, A_DK), BF16)
    row_sub = lax.broadcasted_iota(jnp.int32, (sub, A_DK), 0)
    row_c = lax.broadcasted_iota(jnp.int32, (c_len, A_DK), 0)

    def chunk(c, carry):
        r0 = pl.multiple_of(c * c_len, c_len)
        rows = pl.ds(r0, c_len)
        q = _silu(q_ref[rows, :])
        fz = f_ref[rows, :]
        v = v_ref[rows, :]
        f = lb + one_m_lb * jax.nn.sigmoid(fz)
        logf = jnp.log(jnp.maximum(f, TINY))
        kk = one_m_lb * jax.nn.sigmoid(-fz)
        hi, mid, lo = _split3(logf)
        b = (jnp.dot(tri, hi, preferred_element_type=F32)
             + jnp.dot(tri, mid, preferred_element_type=F32)
             + jnp.dot(tri, lo, preferred_element_type=F32))
        b_ref[...] = b
        k_ref[...] = kk
        st = st_ref[...]
        v_b = v.astype(BF16)

        o_inter = lax.dot_general((q * jnp.exp(b)).astype(BF16), st.astype(BF16), _NT,
                                  preferred_element_type=F32)
        outs = []
        for i in range(c_len // sub):
            lo_r = i * sub
            bi = b[lo_r:lo_r + sub]
            qi = q[lo_r:lo_r + sub]
            es = []
            for s in range(sub):
                bs = b_ref[pl.ds(lo_r + s, 1), :]
                ks = k_ref[pl.ds(lo_r + s, 1), :]
                e = jnp.exp(jnp.where(row_sub >= s, bi - bs, NEG_INF))
                es.append(qi * e * ks)
            e_all = jnp.concatenate(es, axis=0).astype(BF16)
            att = jnp.dot(e_all, ones_kk, preferred_element_type=F32)
            o_i = att[0:sub] * v_ref[pl.ds(r0 + lo_r, 1), :]
            for s in range(1, sub):
                o_i = o_i + att[s * sub:(s + 1) * sub] * v_ref[pl.ds(r0 + lo_r + s, 1), :]
            if i > 0:
                m_row = b_ref[pl.ds(lo_r - 1, 1), :]
                q_t = (qi * jnp.exp(bi - m_row)).astype(BF16)
                k_t = (kk * jnp.exp(jnp.where(row_c < lo_r, m_row - b, NEG_INF))).astype(BF16)
                a_off = lax.dot_general(q_t, k_t, _NT, preferred_element_type=F32)
                o_i = o_i + jnp.dot(a_off.astype(BF16), v_b, preferred_element_type=F32)
            outs.append(o_i)
        o = o_inter + jnp.concatenate(outs, axis=0)

        b_last = b_ref[pl.ds(c_len - 1, 1), :]
        k_dec = (kk * jnp.exp(b_last - b)).astype(BF16)
        st_ref[...] = st * jnp.exp(b_last) + lax.dot_general(v_b, k_dec, _TN, preferred_element_type=F32)

        o = o * lax.rsqrt(jnp.mean(o * o, axis=-1, keepdims=True) + NORM_EPS) * gain
        o_ref[rows, :] = (o * _silu(g_ref[rows, :])).astype(o_ref.dtype)
        return carry

    lax.fori_loop(0, seq // c_len, chunk, 0)


def hgrn2(p, lb_logits, gain, layer, bsz, seq, col_q, col_f, col_v, col_g):
    n_layers = lb_logits.shape[0]
    blk = lambda c0: pl.BlockSpec((seq, A_DK), lambda b, h: (b, c0 + h))
    return pl.pallas_call(
        functools.partial(_hgrn2_kernel, layer=layer, seq=seq),
        grid=(bsz, A_HEADS),
        in_specs=[
            pl.BlockSpec((n_layers, A_DK), lambda b, h: (0, h)),
            pl.BlockSpec((1, A_DV), lambda b, h: (0, h)),
            blk(col_q), blk(col_f), blk(col_v), blk(col_g),
        ],
        out_specs=pl.BlockSpec((seq, A_DV), lambda b, h: (b, h)),
        out_shape=jax.ShapeDtypeStruct((bsz * seq, A_HEADS * A_DV), BF16),
        scratch_shapes=[pltpu.VMEM((A_DV, A_DK), F32), pltpu.VMEM((HG_CHUNK, A_DK), F32),
                        pltpu.VMEM((HG_CHUNK, A_DK), F32)],
        compiler_params=_params("parallel", "parallel"),
        name="hgrn2",
    )(lb_logits, gain.reshape(1, -1), p, p, p, p)


def _compress_kernel(u_ref, pe_ref, w1_ref, w2_ref, o_ref):
    u = u_ref[0, 0]
    half = u.shape[1]
    h_a = jnp.dot((u + pe_ref[0:1, :]).astype(BF16), w1_ref[0:half, :], preferred_element_type=F32)
    h_b = jnp.dot((u + pe_ref[1:2, :]).astype(BF16), w1_ref[half:2 * half, :], preferred_element_type=F32)
    n_units = u.shape[0]
    hid = _silu(h_a + pltpu.roll(h_b, n_units - 1, 0))
    o_ref[0, 0] = jnp.dot(hid.astype(BF16), w2_ref[...], preferred_element_type=F32)


def compress(u, pe, w1, w2):
    bsz, g, n_units, width = u.shape
    return pl.pallas_call(
        _compress_kernel,
        grid=(bsz, g),
        in_specs=[
            pl.BlockSpec((1, 1, n_units, width), lambda b, k: (b, k, 0, 0)),
            pl.BlockSpec((2, width), lambda b, k: (0, 0)),
            pl.BlockSpec((2 * width, B_DH), lambda b, k: (0, 0)),
            pl.BlockSpec((B_DH, B_DH), lambda b, k: (0, 0)),
        ],
        out_specs=pl.BlockSpec((1, 1, n_units, B_DH), lambda b, k: (b, k, 0, 0)),
        out_shape=jax.ShapeDtypeStruct((bsz, g, n_units, B_DH), F32),
        compiler_params=_params("parallel", "parallel"),
        name="nsa_compress",
    )(u, pe.reshape(2, width), w1, w2)


def _nsa_kernel(slopes_ref, q_ref, kc_ref, vc_ref, ks_ref, vs_ref, kw_ref, vw_ref, bg_ref, gate_ref,
                o_ref, score_ref, *, seq):
    tq, kb_len, hpg, dh = NSA_TQ, NSA_KB, B_HPG, B_DH
    rows_all = hpg * tq
    grp = pl.program_id(1)
    qi = pl.program_id(2)
    t0 = qi * tq
    scale = dh ** -0.5

    qf = q_ref[...]
    q_all = jnp.concatenate([qf[:, h * dh:(h + 1) * dh] for h in range(hpg)], axis=0).astype(BF16)
    slope_col = jnp.concatenate(
        [jnp.full((tq, 1), slopes_ref[grp * hpg + h], F32) for h in range(hpg)], axis=0)
    t_loc = lax.broadcasted_iota(jnp.int32, (rows_all, 1), 0) & (tq - 1)

    def softmax_rows(s, mask):
        m = jnp.max(s, axis=-1, keepdims=True)
        e = jnp.where(mask, jnp.exp(s - m), 0.0)
        l = jnp.sum(e, axis=-1, keepdims=True)
        return e / jnp.where(l > 0.0, l, 1.0)

    n_cmp_pad = kc_ref.shape[2]
    kc = kc_ref[0, 0].astype(BF16)
    vc = vc_ref[0, 0].astype(BF16)
    c_end = lax.broadcasted_iota(jnp.int32, (1, n_cmp_pad), 1) * CMP_STRIDE + (CMP_LEN - 1) - t0
    mask_c = t_loc >= c_end
    s_c = lax.dot_general(q_all, kc, _NT, preferred_element_type=F32) * scale + slope_col * c_end.astype(F32)
    p_c = softmax_rows(jnp.where(mask_c, s_c, NEG_INF), mask_c)
    o_cmp = jnp.dot(p_c.astype(BF16), vc, preferred_element_type=F32)

    n_slc = seq // SEL_LEN
    p_sum = p_c[0:tq]
    for h in range(1, hpg):
        p_sum = p_sum + p_c[h * tq:(h + 1) * tq]
    jn = lax.broadcasted_iota(jnp.int32, (n_slc, n_cmp_pad), 0) * SEL_LEN
    cn = lax.broadcasted_iota(jnp.int32, (n_slc, n_cmp_pad), 1) * CMP_STRIDE
    n_cmp = (seq - CMP_LEN) // CMP_STRIDE + 1
    ov_t = jnp.where(cn <= jn + (SEL_LEN - 1),
                     jnp.where(cn + (CMP_LEN - 1) >= jn, jnp.where(cn < n_cmp * CMP_STRIDE, 1.0, 0.0), 0.0),
                     0.0).astype(BF16)
    p_hi = p_sum.astype(BF16)
    p_lo = (p_sum - p_hi.astype(F32)).astype(BF16)
    imp_t = (lax.dot_general(ov_t, p_hi, _NT, preferred_element_type=F32)
             + lax.dot_general(ov_t, p_lo, _NT, preferred_element_type=F32))
    blk = lax.broadcasted_iota(jnp.int32, (n_slc, tq), 0)
    cur = lax.shift_right_logical(t0 + lax.broadcasted_iota(jnp.int32, (n_slc, tq), 1), 6)
    forced = jnp.where(blk == 0, 1, jnp.where(blk == cur, 1, jnp.where(blk == cur - 1, 1, 0)))
    score = jnp.where(forced > 0, FORCE_SCORE, jnp.where(blk <= cur, imp_t, NEG_INF))
    score_ref[...] = score
    rank = jnp.zeros((n_slc, tq), jnp.int32)
    for j in range(n_slc):
        sj = score_ref[pl.ds(j, 1), :]
        rank = rank + jnp.where(sj > score, 1, jnp.where(blk > j, jnp.where(sj == score, 1, 0), 0))
    sel_t = jnp.where(rank < SEL_TOPK, 1.0, 0.0)
    sel = jnp.transpose(jnp.concatenate([sel_t, jnp.zeros((tq - n_slc, tq), F32)], axis=0)).astype(BF16)

    tq_col = lax.broadcasted_iota(jnp.int32, (tq, 1), 0)

    def sel_step(kb, carry):
        m, l, acc = carry
        k0 = pl.multiple_of(kb * kb_len, kb_len)
        k_blk = ks_ref[pl.ds(k0, kb_len), :].astype(BF16)
        v_blk = vs_ref[pl.ds(k0, kb_len), :].astype(BF16)
        kpos = k0 - t0 + lax.broadcasted_iota(jnp.int32, (1, kb_len), 1)
        expand = jnp.where(
            lax.shift_right_logical(k0 + lax.broadcasted_iota(jnp.int32, (tq, kb_len), 1), 6)
            == lax.broadcasted_iota(jnp.int32, (tq, kb_len), 0), 1.0, 0.0).astype(BF16)
        picked = jnp.dot(sel, expand, preferred_element_type=F32)
        ok = jnp.where(kpos <= tq_col, picked, 0.0)
        mask = jnp.concatenate([ok] * hpg, axis=0) > 0.5
        s = lax.dot_general(q_all, k_blk, _NT, preferred_element_type=F32) * scale + slope_col * kpos.astype(F32)
        s = jnp.where(mask, s, NEG_INF)
        m_new = jnp.maximum(m, jnp.max(s, axis=-1, keepdims=True))
        alpha = jnp.exp(m - m_new)
        e = jnp.where(mask, jnp.exp(s - m_new), 0.0)
        l = alpha * l + jnp.sum(e, axis=-1, keepdims=True)
        acc = alpha * acc + jnp.dot(e.astype(BF16), v_blk, preferred_element_type=F32)
        return m_new, l, acc

    n_kb = lax.shift_right_logical(t0 + tq + kb_len - 1, int(math.log2(kb_len)))
    init = (jnp.full((rows_all, 1), NEG_INF, F32), jnp.zeros((rows_all, 1), F32), jnp.zeros((rows_all, dh), F32))
    _, l_s, acc_s = lax.fori_loop(0, n_kb, sel_step, init)
    o_sel = acc_s / l_s

    span = WIN + tq
    w0 = pl.multiple_of(jnp.maximum(t0 - WIN, 0), tq)
    k_w = kw_ref[pl.ds(w0, span), :].astype(BF16)
    v_w = vw_ref[pl.ds(w0, span), :].astype(BF16)
    kpos_w = w0 - t0 + lax.broadcasted_iota(jnp.int32, (1, span), 1)
    d_w = t_loc - kpos_w
    mask_w = jnp.where(d_w >= 0, jnp.where(d_w < WIN, 1, 0), 0) > 0
    s_w = lax.dot_general(q_all, k_w, _NT, preferred_element_type=F32) * scale + slope_col * kpos_w.astype(F32)
    p_w = softmax_rows(jnp.where(mask_w, s_w, NEG_INF), mask_w)
    o_win = jnp.dot(p_w.astype(BF16), v_w, preferred_element_type=F32)

    sg = jax.nn.sigmoid(gate_ref[...])
    gcol = lambda br: jnp.concatenate([sg[:, 3 * h + br:3 * h + br + 1] for h in range(hpg)], axis=0)
    o = gcol(0) * o_cmp + gcol(1) * o_sel + gcol(2) * o_win
    bg = bg_ref[...]
    for h in range(hpg):
        o_ref[:, h * dh:(h + 1) * dh] = (o[h * tq:(h + 1) * tq] * _silu(bg[:, h * dh:(h + 1) * dh])).astype(o_ref.dtype)


def nsa(p, kc, vc, bsz, seq, col_q, col_ks, col_vs, col_kw, col_vw, col_bg, col_gate):
    nq = seq // NSA_TQ
    gw = B_HPG * B_DH
    n_cmp_pad = kc.shape[2]
    slopes = jnp.asarray(2.0 ** (-8.0 * np.arange(1, B_HEADS + 1) / B_HEADS), dtype=F32)
    tile = lambda c0, w: pl.BlockSpec((NSA_TQ, w), lambda b, g, i: (b * nq + i, c0 + g))
    full = lambda c0: pl.BlockSpec((seq, B_DH), lambda b, g, i: (b, c0 + g))
    cmp_spec = pl.BlockSpec((1, 1, n_cmp_pad, B_DH), lambda b, g, i: (b, g, 0, 0))
    return pl.pallas_call(
        functools.partial(_nsa_kernel, seq=seq),
        grid=(bsz, B_KV, nq),
        in_specs=[
            pl.BlockSpec(memory_space=pltpu.SMEM),
            tile(col_q, gw), cmp_spec, cmp_spec,
            full(col_ks), full(col_vs), full(col_kw), full(col_vw),
            tile(col_bg, gw), tile(col_gate, B_DH),
        ],
        out_specs=pl.BlockSpec((NSA_TQ, gw), lambda b, g, i: (b * nq + i, g)),
        out_shape=jax.ShapeDtypeStruct((bsz * seq, B_HEADS * B_DH), BF16),
        scratch_shapes=[pltpu.VMEM((seq // SEL_LEN, NSA_TQ), F32)],
        compiler_params=_params("parallel", "parallel", "arbitrary"),
        name="nsa_attention",
    )(slopes, p, kc, vc, p, p, p, p, p, p)


def _softplus(x):
    return jnp.maximum(x, 0.0) + jnp.log1p(jnp.exp(-jnp.abs(x)))


def _rglru_kernel(xb_ref, g_ref, cw_ref, cb_ref, wa_ref, ba_ref, wi_ref, bi_ref, lam_ref, o_ref,
                  xpad_ref, h_ref):
    tb = RG_TB
    pad = 8

    @pl.when(pl.program_id(2) == 0)
    def _():
        xpad_ref[0:pad, :] = jnp.zeros((pad, RG_BS), F32)
        h_ref[...] = jnp.zeros_like(h_ref)

    x = xb_ref[...]
    xpad_ref[pad:pad + tb, :] = x
    xc = xpad_ref[pl.ds(pad - 3, tb), :] * cw_ref[0:1, :]
    for j in range(1, CONV_W):
        xc = xc + xpad_ref[pl.ds(pad - 3 + j, tb), :] * cw_ref[j:j + 1, :]
    xc = xc + cb_ref[...]
    xpad_ref[0:pad, :] = x[tb - pad:tb]

    xc_b = xc.astype(BF16)
    r = jax.nn.sigmoid(jnp.dot(xc_b, wa_ref[0], preferred_element_type=F32) + ba_ref[...])
    i = jax.nn.sigmoid(jnp.dot(xc_b, wi_ref[0], preferred_element_type=F32) + bi_ref[...])
    log_a = -RG_C * _softplus(-lam_ref[...]) * r
    a = jnp.exp(log_a)
    th = jnp.tanh(log_a)
    u = jnp.sqrt(jnp.maximum(-2.0 * th / (1.0 - th), 0.0)) * (i * xc)

    row = lax.broadcasted_iota(jnp.int32, (tb, RG_BS), 0)
    sh = 1
    while sh < tb:
        keep = row >= sh
        a_s = jnp.where(keep, pltpu.roll(a, sh, 0), 1.0)
        u_s = jnp.where(keep, pltpu.roll(u, sh, 0), 0.0)
        u = a * u_s + u
        a = a * a_s
        sh *= 2
    h = u + a * h_ref[...]
    h_ref[...] = h[tb - 1:tb]
    o_ref[...] = (h * _silu(g_ref[...])).astype(o_ref.dtype)


def rglru(p, conv_w, conv_b, w_a, b_a, w_i, b_i, lam, bsz, seq):
    nt = seq // RG_TB
    d_rnn = RG_BLOCKS * RG_BS
    tile = lambda c0: pl.BlockSpec((RG_TB, RG_BS), lambda b, n, t: (b * nt + t, c0 + n))
    vec = lambda rows: pl.BlockSpec((rows, RG_BS), lambda b, n, t: (0, n))
    mat = pl.BlockSpec((1, RG_BS, RG_BS), lambda b, n, t: (n, 0, 0))
    return pl.pallas_call(
        _rglru_kernel,
        grid=(bsz, RG_BLOCKS, nt),
        in_specs=[tile(0), tile(RG_BLOCKS), vec(CONV_W), vec(1), mat, vec(1), mat, vec(1), vec(1)],
        out_specs=pl.BlockSpec((RG_TB, RG_BS), lambda b, n, t: (b * nt + t, n)),
        out_shape=jax.ShapeDtypeStruct((bsz * seq, d_rnn), BF16),
        scratch_shapes=[pltpu.VMEM((RG_TB + 8, RG_BS), F32), pltpu.VMEM((1, RG_BS), F32)],
        compiler_params=_params("parallel", "parallel", "arbitrary"),
        name="rglru",
    )(p, p, conv_w, conv_b.reshape(1, d_rnn), w_a, b_a.reshape(1, d_rnn), w_i, b_i.reshape(1, d_rnn),
      lam.reshape(1, d_rnn))


def _even_in_weight(w, d_a, d_b):
    a_qk = A_HEADS * A_DK
    kvw = B_KV * B_DH
    sizes = (a_qk, a_qk, d_a, d_a, d_b, kvw, kvw, kvw, kvw, kvw, kvw, 3 * B_HEADS, d_b)
    segs = jnp.split(w, np.cumsum(sizes)[:-1].tolist(), axis=1)
    (a_q, a_f, a_i, a_g, b_q, b_kc, b_vc, b_ks, b_vs, b_kw, b_vw, b_gate, b_g) = segs
    gate = b_gate.reshape(w.shape[0], B_KV, 3 * B_HPG)
    gate = jnp.pad(gate, ((0, 0), (0, 0), (0, B_DH - 3 * B_HPG))).reshape(w.shape[0], B_KV * B_DH)
    order = (("a_q", a_q), ("a_f", a_f), ("a_i", a_i), ("a_g", a_g), ("b_q", b_q), ("b_kc", b_kc),
             ("b_vc", b_vc), ("b_ks", b_ks), ("b_vs", b_vs), ("b_kw", b_kw), ("b_vw", b_vw), ("b_g", b_g),
             ("b_gate", gate))
    offs, at = {}, 0
    for name, seg in order:
        offs[name] = at
        at += seg.shape[1]
    return jnp.concatenate([seg for _, seg in order], axis=1).astype(BF16), offs


def _compress_units(p, off, bsz, seq):
    a = p[:, off:off + B_KV * B_DH].reshape(bsz, seq // CMP_STRIDE, CMP_STRIDE, B_KV, B_DH)
    return a.transpose(0, 3, 1, 2, 4).reshape(bsz, B_KV, seq // CMP_STRIDE, CMP_STRIDE * B_DH)


def kernel(x, norm_w, final_norm_w, even_w_in, even_w_out, hgrn_lb_logits, hgrn_norm_w, cmp_pe_k, cmp_w1_k, cmp_w2_k, cmp_pe_v, cmp_w1_v, cmp_w2_v, odd_w_in, odd_w_out, rg_conv_w, rg_conv_b, rg_w_a, rg_b_a, rg_w_i, rg_b_i, rg_lambda):
    bsz, seq, d_model = x.shape
    depth = norm_w.shape[0]
    d_a = A_HEADS * A_DV
    d_b = B_HEADS * B_DH
    d_rnn = RG_BLOCKS * RG_BS
    assert seq % NSA_KB == 0 and seq % RG_TB == 0 and seq % HG_CHUNK == 0 and CMP_LEN == 2 * CMP_STRIDE
    xf = x.reshape(bsz * seq, d_model)
    for layer in range(depth):
        if layer % 2 == 0:
            e = layer // 2
            w_in, off = _even_in_weight(even_w_in[e], d_a, d_b)
            p = norm_matmul(xf, norm_w[layer], w_in)
            ya = hgrn2(p, hgrn_lb_logits, hgrn_norm_w[e], e, bsz, seq,
                       off["a_q"] // A_DK, off["a_f"] // A_DK, off["a_i"] // A_DV, off["a_g"] // A_DV)
            kc = compress(_compress_units(p, off["b_kc"], bsz, seq), cmp_pe_k[e],
                          cmp_w1_k[e].astype(BF16), cmp_w2_k[e].astype(BF16))
            vc = compress(_compress_units(p, off["b_vc"], bsz, seq), cmp_pe_v[e],
                          cmp_w1_v[e].astype(BF16), cmp_w2_v[e].astype(BF16))
            gw = B_HPG * B_DH
            yb = nsa(p, kc, vc, bsz, seq, off["b_q"] // gw, off["b_ks"] // B_DH, off["b_vs"] // B_DH,
                     off["b_kw"] // B_DH, off["b_vw"] // B_DH, off["b_g"] // gw, off["b_gate"] // B_DH)
            w_out = even_w_out[e].astype(BF16)
            xf = proj_residual([ya, yb], [w_out[:d_a], w_out[d_a:]], xf)
        else:
            o = layer // 2
            p = norm_matmul(xf, norm_w[layer], odd_w_in[o].astype(BF16))
            hr = rglru(p, rg_conv_w[o], rg_conv_b[o], rg_w_a[o].astype(BF16), rg_b_a[o],
                       rg_w_i[o].astype(BF16), rg_b_i[o], rg_lambda[o], bsz, seq)
            xf = proj_residual([hr], [odd_w_out[o].astype(BF16)], xf)
    return rmsnorm(xf, final_norm_w).reshape(bsz, seq, d_model)
```

```python
import functools
import math

import jax
import jax.numpy as jnp
import numpy as np
from jax import lax
from jax.experimental import pallas as pl
from jax.experimental.pallas import tpu as pltpu

F32 = jnp.float32
BF16 = jnp.bfloat16

NORM_EPS = 1e-6
NEG_INF = -1e30
FORCE_SCORE = 1e30
TINY = 1e-30

A_HEADS = 16
A_DK = 128
A_DV = 128
HG_CHUNK = 64
HG_SUB = 8

B_HEADS = 16
B_DH = 128
B_KV = 4
B_HPG = B_HEADS // B_KV
CMP_LEN = 32
CMP_STRIDE = 16
SEL_LEN = 64
SEL_TOPK = 16
WIN = 512
NSA_TQ = 128
NSA_KB = 512

RG_BLOCKS = 10
RG_BS = 256
CONV_W = 4
RG_C = 8.0
RG_TB = 256

VMEM_LIMIT = 48 * 1024 * 1024

_NT = (((1,), (1,)), ((), ()))
_TN = (((0,), (0,)), ((), ()))


def _silu(x):
    return x * jax.nn.sigmoid(x)


def _params(*sem):
    return pltpu.CompilerParams(dimension_semantics=sem, vmem_limit_bytes=VMEM_LIMIT)


def _norm_matmul_kernel(x_ref, nw_ref, w_ref, o_ref, h_ref):
    @pl.when(pl.program_id(1) == 0)
    def _():
        x = x_ref[...]
        ms = jnp.mean(x * x, axis=-1, keepdims=True)
        h_ref[...] = (x * lax.rsqrt(ms + NORM_EPS) * nw_ref[...]).astype(BF16)

    o_ref[...] = jnp.dot(h_ref[...], w_ref[...], preferred_element_type=F32)


def norm_matmul(x, nw, w, tm=1024, tn=512):
    m, d = x.shape
    n = w.shape[1]
    return pl.pallas_call(
        _norm_matmul_kernel,
        grid=(m // tm, n // tn),
        in_specs=[
            pl.BlockSpec((tm, d), lambda i, j: (i, 0)),
            pl.BlockSpec((1, d), lambda i, j: (0, 0)),
            pl.BlockSpec((d, tn), lambda i, j: (0, j)),
        ],
        out_specs=pl.BlockSpec((tm, tn), lambda i, j: (i, j)),
        out_shape=jax.ShapeDtypeStruct((m, n), F32),
        scratch_shapes=[pltpu.VMEM((tm, d), BF16)],
        compiler_params=_params("parallel", "arbitrary"),
        name="norm_matmul",
    )(x, nw.reshape(1, d), w)


def _proj_residual_kernel(*refs, n_in):
    a_refs, w_refs = refs[:n_in], refs[n_in:2 * n_in]
    x_ref, o_ref = refs[2 * n_in], refs[2 * n_in + 1]
    y = jnp.dot(a_refs[0][...], w_refs[0][...], preferred_element_type=F32)
    for a_ref, w_ref in zip(a_refs[1:], w_refs[1:]):
        y = y + jnp.dot(a_ref[...], w_ref[...], preferred_element_type=F32)
    o_ref[...] = x_ref[...] + y


def proj_residual(acts, ws, x, tm=1024, tn=512):
    m, n = x.shape
    n_in = len(acts)
    in_specs = [pl.BlockSpec((tm, a.shape[1]), lambda i, j: (i, 0)) for a in acts]
    in_specs += [pl.BlockSpec((w.shape[0], tn), lambda i, j: (0, j)) for w in ws]
    in_specs += [pl.BlockSpec((tm, tn), lambda i, j: (i, j))]
    return pl.pallas_call(
        functools.partial(_proj_residual_kernel, n_in=n_in),
        grid=(m // tm, n // tn),
        in_specs=in_specs,
        out_specs=pl.BlockSpec((tm, tn), lambda i, j: (i, j)),
        out_shape=jax.ShapeDtypeStruct((m, n), F32),
        compiler_params=_params("parallel", "arbitrary"),
        name="proj_residual",
    )(*acts, *ws, x)


def _rmsnorm_kernel(x_ref, w_ref, o_ref):
    x = x_ref[...]
    ms = jnp.mean(x * x, axis=-1, keepdims=True)
    o_ref[...] = x * lax.rsqrt(ms + NORM_EPS) * w_ref[...]


def rmsnorm(x, w, tm=512):
    m, d = x.shape
    return pl.pallas_call(
        _rmsnorm_kernel,
        grid=(m // tm,),
        in_specs=[pl.BlockSpec((tm, d), lambda i: (i, 0)), pl.BlockSpec((1, d), lambda i: (0, 0))],
        out_specs=pl.BlockSpec((tm, d), lambda i: (i, 0)),
        out_shape=jax.ShapeDtypeStruct((m, d), F32),
        compiler_params=_params("parallel"),
        name="final_rmsnorm",
    )(x, w.reshape(1, d))


def _split3(x):
    hi = x.astype(BF16)
    r1 = x - hi.astype(F32)
    mid = r1.astype(BF16)
    lo = (r1 - mid.astype(F32)).astype(BF16)
    return hi, mid, lo


def _hgrn2_levels(c_len, sub):
    levels, seg, half = [], 0, sub
    while half < c_len:
        levels.append((half, seg))
        seg += c_len // (2 * half)
        half *= 2
    return levels, seg


def _hgrn2_kernel(lbl_ref, gain_ref, q_ref, f_ref, v_ref, g_ref, o_ref, st_ref, b_ref, k_ref, vc_ref, *,
                  layer, hb, tblk):
    c_len, sub = HG_CHUNK, HG_SUB
    n_sub = c_len // sub
    levels, n_seg = _hgrn2_levels(c_len, sub)
    lg = lbl_ref[...]
    ex = jnp.exp(lg - jnp.max(lg, axis=0, keepdims=True))
    sm = ex / jnp.sum(ex, axis=0, keepdims=True)
    lb_all = sm[0:1]
    for j in range(1, layer + 1):
        lb_all = lb_all + sm[j:j + 1]
    lb_all = lb_all - sm[0:1]
    gain_all = gain_ref[...]

    @pl.when(pl.program_id(2) == 0)
    def _():
        st_ref[...] = jnp.zeros_like(st_ref)

    tri = jnp.where(lax.broadcasted_iota(jnp.int32, (c_len, c_len), 0)
                    >= lax.broadcasted_iota(jnp.int32, (c_len, c_len), 1), 1.0, 0.0).astype(BF16)
    ones_kk = jnp.ones((A_DK, A_DK), BF16)
    row_sub = lax.broadcasted_iota(jnp.int32, (sub, A_DK), 0)
    zero_blk = jnp.zeros((sub, A_DK), F32)

    def head_chunk(h, r0):
        rows = pl.ds(r0, c_len)
        cols = slice(h * A_DK, (h + 1) * A_DK)
        lb = lb_all[:, cols]
        one_m_lb = 1.0 - lb
        q = _silu(q_ref[rows, cols])
        fz = f_ref[rows, cols]
        v = v_ref[rows, cols]
        vc_ref[h] = v
        v_b = v.astype(BF16)
        f = lb + one_m_lb * jax.nn.sigmoid(fz)
        hi, mid, lo = _split3(jnp.log2(jnp.maximum(f, TINY)))
        kk = one_m_lb * jax.nn.sigmoid(-fz)
        b = (jnp.dot(tri, hi, preferred_element_type=F32)
             + jnp.dot(tri, mid, preferred_element_type=F32)
             + jnp.dot(tri, lo, preferred_element_type=F32))
        b_ref[h] = b
        k_ref[h] = kk
        st = st_ref[h]
        o_inter = lax.dot_general((q * jnp.exp2(b)).astype(BF16), st.astype(BF16), _NT,
                                  preferred_element_type=F32)

        es = []
        for j in range(n_sub):
            bj = b[j * sub:(j + 1) * sub]
            qj = q[j * sub:(j + 1) * sub]
            for s in range(sub):
                bs = b_ref[h, pl.ds(j * sub + s, 1), :]
                ks = k_ref[h, pl.ds(j * sub + s, 1), :]
                es.append(qj * jnp.exp2(jnp.where(row_sub >= s, bj - bs, NEG_INF)) * ks)
        att_d = jnp.dot(jnp.concatenate(es, axis=0).astype(BF16), ones_kk, preferred_element_type=F32)
        o_diag = []
        for j in range(n_sub):
            acc = None
            for s in range(sub):
                u = j * sub + s
                term = att_d[u * sub:(u + 1) * sub] * vc_ref[h, pl.ds(u, 1), :]
                acc = term if acc is None else acc + term
            o_diag.append(acc)

        q_rows, k_rows = [], []
        for j in range(n_sub):
            bj = b[j * sub:(j + 1) * sub]
            q_seg = [zero_blk] * n_seg
            k_seg = [zero_blk] * n_seg
            for half, seg0 in levels:
                blk = (j * sub) // (2 * half)
                m_row = b_ref[h, pl.ds(blk * 2 * half + half - 1, 1), :]
                if ((j * sub) // half) % 2 == 1:
                    q_seg[seg0 + blk] = q[j * sub:(j + 1) * sub] * jnp.exp2(bj - m_row)
                else:
                    k_seg[seg0 + blk] = kk[j * sub:(j + 1) * sub] * jnp.exp2(m_row - bj)
            q_rows.append(jnp.concatenate(q_seg, axis=1))
            k_rows.append(jnp.concatenate(k_seg, axis=1))
        att_o = lax.dot_general(jnp.concatenate(q_rows, axis=0).astype(BF16),
                                jnp.concatenate(k_rows, axis=0).astype(BF16), _NT,
                                preferred_element_type=F32)
        o = (o_inter + jnp.dot(att_o.astype(BF16), v_b, preferred_element_type=F32)
             + jnp.concatenate(o_diag, axis=0))

        b_last = b_ref[h, pl.ds(c_len - 1, 1), :]
        k_dec = (kk * jnp.exp2(b_last - b)).astype(BF16)
        st_ref[h] = st * jnp.exp2(b_last) + lax.dot_general(v_b, k_dec, _TN, preferred_element_type=F32)

        o = o * lax.rsqrt(jnp.mean(o * o, axis=-1, keepdims=True) + NORM_EPS) * gain_all[:, cols]
        o_ref[rows, cols] = (o * _silu(g_ref[rows, cols])).astype(o_ref.dtype)

    def chunk(c, carry):
        r0 = pl.multiple_of(c * c_len, c_len)
        for h in range(hb):
            head_chunk(h, r0)
        return carry

    lax.fori_loop(0, tblk // c_len, chunk, 0)


def hgrn2(p, lb_logits, gain, layer, bsz, seq, col_q, col_f, col_v, col_g, hb=8, tblk=256):
    n_layers = lb_logits.shape[0]
    nt = seq // tblk
    w = hb * A_DK
    blk = lambda c0: pl.BlockSpec((tblk, w), lambda b, h, t: (b * nt + t, c0 // hb + h))
    return pl.pallas_call(
        functools.partial(_hgrn2_kernel, layer=layer, hb=hb, tblk=tblk),
        grid=(bsz, A_HEADS // hb, nt),
        in_specs=[
            pl.BlockSpec((n_layers, w), lambda b, h, t: (0, h)),
            pl.BlockSpec((1, w), lambda b, h, t: (0, h)),
            blk(col_q), blk(col_f), blk(col_v), blk(col_g),
        ],
        out_specs=pl.BlockSpec((tblk, w), lambda b, h, t: (b * nt + t, h)),
        out_shape=jax.ShapeDtypeStruct((bsz * seq, A_HEADS * A_DV), BF16),
        scratch_shapes=[pltpu.VMEM((hb, A_DV, A_DK), F32), pltpu.VMEM((hb, HG_CHUNK, A_DK), F32),
                        pltpu.VMEM((hb, HG_CHUNK, A_DK), F32), pltpu.VMEM((hb, HG_CHUNK, A_DV), F32)],
        compiler_params=_params("parallel", "parallel", "arbitrary"),
        name="hgrn2",
    )(lb_logits, gain.reshape(1, -1), p, p, p, p)


def _compress_kernel(u_ref, pe_ref, w1_ref, w2_ref, o_ref):
    u = u_ref[0, 0]
    half = u.shape[1]
    h_a = jnp.dot((u + pe_ref[0:1, :]).astype(BF16), w1_ref[0:half, :], preferred_element_type=F32)
    h_b = jnp.dot((u + pe_ref[1:2, :]).astype(BF16), w1_ref[half:2 * half, :], preferred_element_type=F32)
    n_units = u.shape[0]
    hid = _silu(h_a + pltpu.roll(h_b, n_units - 1, 0))
    o_ref[0, 0] = jnp.dot(hid.astype(BF16), w2_ref[...], preferred_element_type=F32)


def compress(u, pe, w1, w2):
    bsz, g, n_units, width = u.shape
    return pl.pallas_call(
        _compress_kernel,
        grid=(bsz, g),
        in_specs=[
            pl.BlockSpec((1, 1, n_units, width), lambda b, k: (b, k, 0, 0)),
            pl.BlockSpec((2, width), lambda b, k: (0, 0)),
            pl.BlockSpec((2 * width, B_DH), lambda b, k: (0, 0)),
            pl.BlockSpec((B_DH, B_DH), lambda b, k: (0, 0)),
        ],
        out_specs=pl.BlockSpec((1, 1, n_units, B_DH), lambda b, k: (b, k, 0, 0)),
        out_shape=jax.ShapeDtypeStruct((bsz, g, n_units, B_DH), F32),
        compiler_params=_params("parallel", "parallel"),
        name="nsa_compress",
    )(u, pe.reshape(2, width), w1, w2)


def _nsa_kernel(slopes_ref, q_ref, kc_ref, vc_ref, ks_ref, vs_ref, kw_ref, vw_ref, bg_ref, gate_ref,
                o_ref, score_ref, *, seq):
    tq, kb_len, hpg, dh = NSA_TQ, NSA_KB, B_HPG, B_DH
    rows_all = hpg * tq
    grp = pl.program_id(1)
    qi = pl.program_id(2)
    t0 = qi * tq
    scale = dh ** -0.5

    qf = q_ref[...]
    q_all = jnp.concatenate([qf[:, h * dh:(h + 1) * dh] for h in range(hpg)], axis=0).astype(BF16)
    slope_col = jnp.concatenate(
        [jnp.full((tq, 1), slopes_ref[grp * hpg + h], F32) for h in range(hpg)], axis=0)
    t_loc = lax.broadcasted_iota(jnp.int32, (rows_all, 1), 0) & (tq - 1)

    def softmax_rows(s, mask):
        m = jnp.max(s, axis=-1, keepdims=True)
        e = jnp.where(mask, jnp.exp(s - m), 0.0)
        l = jnp.sum(e, axis=-1, keepdims=True)
        return e / jnp.where(l > 0.0, l, 1.0)

    n_cmp_pad = kc_ref.shape[2]
    kc = kc_ref[0, 0].astype(BF16)
    vc = vc_ref[0, 0].astype(BF16)
    c_end = lax.broadcasted_iota(jnp.int32, (1, n_cmp_pad), 1) * CMP_STRIDE + (CMP_LEN - 1) - t0
    mask_c = t_loc >= c_end
    s_c = lax.dot_general(q_all, kc, _NT, preferred_element_type=F32) * scale + slope_col * c_end.astype(F32)
    p_c = softmax_rows(jnp.where(mask_c, s_c, NEG_INF), mask_c)
    o_cmp = jnp.dot(p_c.astype(BF16), vc, preferred_element_type=F32)

    n_slc = seq // SEL_LEN
    p_sum = p_c[0:tq]
    for h in range(1, hpg):
        p_sum = p_sum + p_c[h * tq:(h + 1) * tq]
    jn = lax.broadcasted_iota(jnp.int32, (n_slc, n_cmp_pad), 0) * SEL_LEN
    cn = lax.broadcasted_iota(jnp.int32, (n_slc, n_cmp_pad), 1) * CMP_STRIDE
    n_cmp = (seq - CMP_LEN) // CMP_STRIDE + 1
    ov_t = jnp.where(cn <= jn + (SEL_LEN - 1),
                     jnp.where(cn + (CMP_LEN - 1) >= jn, jnp.where(cn < n_cmp * CMP_STRIDE, 1.0, 0.0), 0.0),
                     0.0).astype(BF16)
    p_hi = p_sum.astype(BF16)
    p_lo = (p_sum - p_hi.astype(F32)).astype(BF16)
    imp_t = (lax.dot_general(ov_t, p_hi, _NT, preferred_element_type=F32)
             + lax.dot_general(ov_t, p_lo, _NT, preferred_element_type=F32))
    blk = lax.broadcasted_iota(jnp.int32, (n_slc, tq), 0)
    cur = lax.shift_right_logical(t0 + lax.broadcasted_iota(jnp.int32, (n_slc, tq), 1), 6)
    forced = jnp.where(blk == 0, 1, jnp.where(blk == cur, 1, jnp.where(blk == cur - 1, 1, 0)))
    score = jnp.where(forced > 0, FORCE_SCORE, jnp.where(blk <= cur, imp_t, NEG_INF))
    score_ref[...] = score
    rank = jnp.zeros((n_slc, tq), jnp.int32)
    for j in range(n_slc):
        sj = score_ref[pl.ds(j, 1), :]
        rank = rank + jnp.where(sj > score, 1, jnp.where(blk > j, jnp.where(sj == score, 1, 0), 0))
    sel_t = jnp.where(rank < SEL_TOPK, 1.0, 0.0)
    sel = jnp.transpose(jnp.concatenate([sel_t, jnp.zeros((tq - n_slc, tq), F32)], axis=0)).astype(BF16)

    tq_col = lax.broadcasted_iota(jnp.int32, (tq, 1), 0)

    def sel_step(kb, carry):
        m, l, acc = carry
        k0 = pl.multiple_of(kb * kb_len, kb_len)
        k_blk = ks_ref[pl.ds(k0, kb_len), :].astype(BF16)
        v_blk = vs_ref[pl.ds(k0, kb_len), :].astype(BF16)
        kpos = k0 - t0 + lax.broadcasted_iota(jnp.int32, (1, kb_len), 1)
        expand = jnp.where(
            lax.shift_right_logical(k0 + lax.broadcasted_iota(jnp.int32, (tq, kb_len), 1), 6)
            == lax.broadcasted_iota(jnp.int32, (tq, kb_len), 0), 1.0, 0.0).astype(BF16)
        picked = jnp.dot(sel, expand, preferred_element_type=F32)
        ok = jnp.where(kpos <= tq_col, picked, 0.0)
        mask = jnp.concatenate([ok] * hpg, axis=0) > 0.5
        s = lax.dot_general(q_all, k_blk, _NT, preferred_element_type=F32) * scale + slope_col * kpos.astype(F32)
        s = jnp.where(mask, s, NEG_INF)
        m_new = jnp.maximum(m, jnp.max(s, axis=-1, keepdims=True))
        alpha = jnp.exp(m - m_new)
        e = jnp.where(mask, jnp.exp(s - m_new), 0.0)
        l = alpha * l + jnp.sum(e, axis=-1, keepdims=True)
        acc = alpha * acc + jnp.dot(e.astype(BF16), v_blk, preferred_element_type=F32)
        return m_new, l, acc

    n_kb = lax.shift_right_logical(t0 + tq + kb_len - 1, int(math.log2(kb_len)))
    init = (jnp.full((rows_all, 1), NEG_INF, F32), jnp.zeros((rows_all, 1), F32), jnp.zeros((rows_all, dh), F32))
    _, l_s, acc_s = lax.fori_loop(0, n_kb, sel_step, init)
    o_sel = acc_s / l_s

    span = WIN + tq
    w0 = pl.multiple_of(jnp.maximum(t0 - WIN, 0), tq)
    k_w = kw_ref[pl.ds(w0, span), :].astype(BF16)
    v_w = vw_ref[pl.ds(w0, span), :].astype(BF16)
    kpos_w = w0 - t0 + lax.broadcasted_iota(jnp.int32, (1, span), 1)
    d_w = t_loc - kpos_w
    mask_w = jnp.where(d_w >= 0, jnp.where(d_w < WIN, 1, 0), 0) > 0
    s_w = lax.dot_general(q_all, k_w, _NT, preferred_element_type=F32) * scale + slope_col * kpos_w.astype(F32)
    p_w = softmax_rows(jnp.where(mask_w, s_w, NEG_INF), mask_w)
    o_win = jnp.dot(p_w.astype(BF16), v_w, preferred_element_type=F32)

    sg = jax.nn.sigmoid(gate_ref[...])
    gcol = lambda br: jnp.concatenate([sg[:, 3 * h + br:3 * h + br + 1] for h in range(hpg)], axis=0)
    o = gcol(0) * o_cmp + gcol(1) * o_sel + gcol(2) * o_win
    bg = bg_ref[...]
    for h in range(hpg):
        o_ref[:, h * dh:(h + 1) * dh] = (o[h * tq:(h + 1) * tq] * _silu(bg[:, h * dh:(h + 1) * dh])).astype(o_ref.dtype)


def nsa(p, kc, vc, bsz, seq, col_q, col_ks, col_vs, col_kw, col_vw, col_bg, col_gate):
    nq = seq // NSA_TQ
    gw = B_HPG * B_DH
    n_cmp_pad = kc.shape[2]
    slopes = jnp.asarray(2.0 ** (-8.0 * np.arange(1, B_HEADS + 1) / B_HEADS), dtype=F32)
    tile = lambda c0, w: pl.BlockSpec((NSA_TQ, w), lambda b, g, i: (b * nq + i, c0 + g))
    full = lambda c0: pl.BlockSpec((seq, B_DH), lambda b, g, i: (b, c0 + g))
    cmp_spec = pl.BlockSpec((1, 1, n_cmp_pad, B_DH), lambda b, g, i: (b, g, 0, 0))
    return pl.pallas_call(
        functools.partial(_nsa_kernel, seq=seq),
        grid=(bsz, B_KV, nq),
        in_specs=[
            pl.BlockSpec(memory_space=pltpu.SMEM),
            tile(col_q, gw), cmp_spec, cmp_spec,
            full(col_ks), full(col_vs), full(col_kw), full(col_vw),
            tile(col_bg, gw), tile(col_gate, B_DH),
        ],
        out_specs=pl.BlockSpec((NSA_TQ, gw), lambda b, g, i: (b * nq + i, g)),
        out_shape=jax.ShapeDtypeStruct((bsz * seq, B_HEADS * B_DH), BF16),
        scratch_shapes=[pltpu.VMEM((seq // SEL_LEN, NSA_TQ), F32)],
        compiler_params=_params("parallel", "parallel", "arbitrary"),
        name="nsa_attention",
    )(slopes, p, kc, vc, p, p, p, p, p, p)


def _softplus(x):
    return jnp.maximum(x, 0.0) + jnp.log1p(jnp.exp(-jnp.abs(x)))


def _rglru_kernel(xb_ref, g_ref, cw_ref, cb_ref, wa_ref, ba_ref, wi_ref, bi_ref, lam_ref, o_ref,
                  xpad_ref, h_ref):
    tb = RG_TB
    pad = 8

    @pl.when(pl.program_id(2) == 0)
    def _():
        xpad_ref[0:pad, :] = jnp.zeros((pad, RG_BS), F32)
        h_ref[...] = jnp.zeros_like(h_ref)

    x = xb_ref[...]
    xpad_ref[pad:pad + tb, :] = x
    xc = xpad_ref[pl.ds(pad - 3, tb), :] * cw_ref[0:1, :]
    for j in range(1, CONV_W):
        xc = xc + xpad_ref[pl.ds(pad - 3 + j, tb), :] * cw_ref[j:j + 1, :]
    xc = xc + cb_ref[...]
    xpad_ref[0:pad, :] = x[tb - pad:tb]

    xc_b = xc.astype(BF16)
    r = jax.nn.sigmoid(jnp.dot(xc_b, wa_ref[0], preferred_element_type=F32) + ba_ref[...])
    i = jax.nn.sigmoid(jnp.dot(xc_b, wi_ref[0], preferred_element_type=F32) + bi_ref[...])
    log_a = -RG_C * _softplus(-lam_ref[...]) * r
    a = jnp.exp(log_a)
    th = jnp.tanh(log_a)
    u = jnp.sqrt(jnp.maximum(-2.0 * th / (1.0 - th), 0.0)) * (i * xc)

    row = lax.broadcasted_iota(jnp.int32, (tb, RG_BS), 0)
    sh = 1
    while sh < tb:
        keep = row >= sh
        a_s = jnp.where(keep, pltpu.roll(a, sh, 0), 1.0)
        u_s = jnp.where(keep, pltpu.roll(u, sh, 0), 0.0)
        u = a * u_s + u
        a = a * a_s
        sh *= 2
    h = u + a * h_ref[...]
    h_ref[...] = h[tb - 1:tb]
    o_ref[...] = (h * _silu(g_ref[...])).astype(o_ref.dtype)


def rglru(p, conv_w, conv_b, w_a, b_a, w_i, b_i, lam, bsz, seq):
    nt = seq // RG_TB
    d_rnn = RG_BLOCKS * RG_BS
    tile = lambda c0: pl.BlockSpec((RG_TB, RG_BS), lambda b, n, t: (b * nt + t, c0 + n))
    vec = lambda rows: pl.BlockSpec((rows, RG_BS), lambda b, n, t: (0, n))
    mat = pl.BlockSpec((1, RG_BS, RG_BS), lambda b, n, t: (n, 0, 0))
    return pl.pallas_call(
        _rglru_kernel,
        grid=(bsz, RG_BLOCKS, nt),
        in_specs=[tile(0), tile(RG_BLOCKS), vec(CONV_W), vec(1), mat, vec(1), mat, vec(1), vec(1)],
        out_specs=pl.BlockSpec((RG_TB, RG_BS), lambda b, n, t: (b * nt + t, n)),
        out_shape=jax.ShapeDtypeStruct((bsz * seq, d_rnn), BF16),
        scratch_shapes=[pltpu.VMEM((RG_TB + 8, RG_BS), F32), pltpu.VMEM((1, RG_BS), F32)],
        compiler_params=_params("parallel", "parallel", "arbitrary"),
        name="rglru",
    )(p, p, conv_w, conv_b.reshape(1, d_rnn), w_a, b_a.reshape(1, d_rnn), w_i, b_i.reshape(1, d_rnn),
      lam.reshape(1, d_rnn))


def _even_in_weight(w, d_a, d_b):
    a_qk = A_HEADS * A_DK
    kvw = B_KV * B_DH
    sizes = (a_qk, a_qk, d_a, d_a, d_b, kvw, kvw, kvw, kvw, kvw, kvw, 3 * B_HEADS, d_b)
    segs = jnp.split(w, np.cumsum(sizes)[:-1].tolist(), axis=1)
    (a_q, a_f, a_i, a_g, b_q, b_kc, b_vc, b_ks, b_vs, b_kw, b_vw, b_gate, b_g) = segs
    gate = b_gate.reshape(w.shape[0], B_KV, 3 * B_HPG)
    gate = jnp.pad(gate, ((0, 0), (0, 0), (0, B_DH - 3 * B_HPG))).reshape(w.shape[0], B_KV * B_DH)
    order = (("a_q", a_q), ("a_f", a_f), ("a_i", a_i), ("a_g", a_g), ("b_q", b_q), ("b_kc", b_kc),
             ("b_vc", b_vc), ("b_ks", b_ks), ("b_vs", b_vs), ("b_kw", b_kw), ("b_vw", b_vw), ("b_g", b_g),
             ("b_gate", gate))
    offs, at = {}, 0
    for name, seg in order:
        offs[name] = at
        at += seg.shape[1]
    return jnp.concatenate([seg for _, seg in order], axis=1).astype(BF16), offs


def _compress_units(p, off, bsz, seq):
    a = p[:, off:off + B_KV * B_DH].reshape(bsz, seq // CMP_STRIDE, CMP_STRIDE, B_KV, B_DH)
    return a.transpose(0, 3, 1, 2, 4).reshape(bsz, B_KV, seq // CMP_STRIDE, CMP_STRIDE * B_DH)


def kernel(x, norm_w, final_norm_w, even_w_in, even_w_out, hgrn_lb_logits, hgrn_norm_w, cmp_pe_k, cmp_w1_k, cmp_w2_k, cmp_pe_v, cmp_w1_v, cmp_w2_v, odd_w_in, odd_w_out, rg_conv_w, rg_conv_b, rg_w_a, rg_b_a, rg_w_i, rg_b_i, rg_lambda):
    bsz, seq, d_model = x.shape
    depth = norm_w.shape[0]
    d_a = A_HEADS * A_DV
    d_b = B_HEADS * B_DH
    d_rnn = RG_BLOCKS * RG_BS
    assert seq % NSA_KB == 0 and seq % RG_TB == 0 and seq % HG_CHUNK == 0 and CMP_LEN == 2 * CMP_STRIDE
    xf = x.reshape(bsz * seq, d_model)
    for layer in range(depth):
        if layer % 2 == 0:
            e = layer // 2
            w_in, off = _even_in_weight(even_w_in[e], d_a, d_b)
            p = norm_matmul(xf, norm_w[layer], w_in)
            ya = hgrn2(p, hgrn_lb_logits, hgrn_norm_w[e], e, bsz, seq,
                       off["a_q"] // A_DK, off["a_f"] // A_DK, off["a_i"] // A_DV, off["a_g"] // A_DV)
            kc = compress(_compress_units(p, off["b_kc"], bsz, seq), cmp_pe_k[e],
                          cmp_w1_k[e].astype(BF16), cmp_w2_k[e].astype(BF16))
            vc = compress(_compress_units(p, off["b_vc"], bsz, seq), cmp_pe_v[e],
                          cmp_w1_v[e].astype(BF16), cmp_w2_v[e].astype(BF16))
            gw = B_HPG * B_DH
            yb = nsa(p, kc, vc, bsz, seq, off["b_q"] // gw, off["b_ks"] // B_DH, off["b_vs"] // B_DH,
                     off["b_kw"] // B_DH, off["b_vw"] // B_DH, off["b_g"] // gw, off["b_gate"] // B_DH)
            w_out = even_w_out[e].astype(BF16)
            xf = proj_residual([ya, yb], [w_out[:d_a], w_out[d_a:]], xf)
        else:
            o = layer // 2
            p = norm_matmul(xf, norm_w[layer], odd_w_in[o].astype(BF16))
            hr = rglru(p, rg_conv_w[o], rg_conv_b[o], rg_w_a[o].astype(BF16), rg_b_a[o],
                       rg_w_i[o].astype(BF16), rg_b_i[o], rg_lambda[o], bsz, seq)
            xf = proj_residual([hr], [odd_w_out[o].astype(BF16)], xf)
    return rmsnorm(xf, final_norm_w).reshape(bsz, seq, d_model)
```

```python
import functools
import math

import jax
import jax.numpy as jnp
import numpy as np
from jax import lax
from jax.experimental import pallas as pl
from jax.experimental.pallas import tpu as pltpu

F32 = jnp.float32
BF16 = jnp.bfloat16

NORM_EPS = 1e-6
NEG_INF = -1e30
FORCE_SCORE = 1e30
TINY = 1e-30
LOG2E = 1.4426950408889634

A_HEADS = 16
A_DK = 128
A_DV = 128
HG_CHUNK = 64
HG_SUB = 8

B_HEADS = 16
B_DH = 128
B_KV = 4
B_HPG = B_HEADS // B_KV
CMP_LEN = 32
CMP_STRIDE = 16
SEL_LEN = 64
SEL_TOPK = 16
WIN = 512
NSA_TQ = 128
NSA_KB = 512
NSA_GB = 2
MASK_BIG = 32768.0

LANE_ALIBI = 32
LANE_PAD = 38
LANE_CHUNK = 40

RG_BLOCKS = 10
RG_BS = 256
CONV_W = 4
RG_C = 8.0
RG_TB = 256

VMEM_LIMIT = 48 * 1024 * 1024

_NT = (((1,), (1,)), ((), ()))
_TN = (((0,), (0,)), ((), ()))


def _silu(x):
    return x * jax.nn.sigmoid(x)


def _params(*sem):
    return pltpu.CompilerParams(dimension_semantics=sem, vmem_limit_bytes=VMEM_LIMIT)


def _norm_matmul_kernel(x_ref, nw_ref, w_ref, o_ref, h_ref):
    @pl.when(pl.program_id(1) == 0)
    def _():
        x = x_ref[...]
        ms = jnp.mean(x * x, axis=-1, keepdims=True)
        h_ref[...] = (x * lax.rsqrt(ms + NORM_EPS) * nw_ref[...]).astype(BF16)

    o_ref[...] = jnp.dot(h_ref[...], w_ref[...], preferred_element_type=F32)


def norm_matmul(x, nw, w, layer, n_tiles, skip_tile=None, tm=1024, tn=512):
    m, d = x.shape
    src = (lambda j: j) if skip_tile is None else (lambda j: j + j // skip_tile)
    assert skip_tile is None or n_tiles <= 2 * skip_tile
    return pl.pallas_call(
        _norm_matmul_kernel,
        grid=(m // tm, n_tiles),
        in_specs=[
            pl.BlockSpec((tm, d), lambda i, j: (i, 0)),
            pl.BlockSpec((1, d), lambda i, j: (0, 0)),
            pl.BlockSpec((None, d, tn), lambda i, j: (layer, 0, src(j))),
        ],
        out_specs=pl.BlockSpec((tm, tn), lambda i, j: (i, j)),
        out_shape=jax.ShapeDtypeStruct((m, n_tiles * tn), F32),
        scratch_shapes=[pltpu.VMEM((tm, d), BF16)],
        compiler_params=_params("parallel", "arbitrary"),
        name="norm_matmul",
    )(x, nw.reshape(1, d), w)


def _norm_matmul_t_kernel(x_ref, nw_ref, wt_ref, o_ref, h_ref):
    @pl.when(pl.program_id(1) == 0)
    def _():
        x = x_ref[...]
        ms = jnp.mean(x * x, axis=-1, keepdims=True)
        h_ref[...] = (x * lax.rsqrt(ms + NORM_EPS) * nw_ref[...]).astype(BF16)

    o_ref[...] = lax.dot_general(wt_ref[...], h_ref[...], _NT, preferred_element_type=F32).astype(o_ref.dtype)


def norm_matmul_t(x, nw, wt, layer, tm=1024, tn=512):
    m, d = x.shape
    n = wt.shape[1]
    return pl.pallas_call(
        _norm_matmul_t_kernel,
        grid=(m // tm, n // tn),
        in_specs=[
            pl.BlockSpec((tm, d), lambda i, j: (i, 0)),
            pl.BlockSpec((1, d), lambda i, j: (0, 0)),
            pl.BlockSpec((None, tn, d), lambda i, j: (layer, j, 0)),
        ],
        out_specs=pl.BlockSpec((tn, tm), lambda i, j: (j, i)),
        out_shape=jax.ShapeDtypeStruct((n, m), BF16),
        scratch_shapes=[pltpu.VMEM((tm, d), BF16)],
        compiler_params=_params("parallel", "arbitrary"),
        name="norm_matmul_t",
    )(x, nw.reshape(1, d), wt)


def _proj_residual_kernel(*refs, n_in):
    a_refs, w_refs = refs[:n_in], refs[n_in:2 * n_in]
    x_ref, o_ref = refs[2 * n_in], refs[2 * n_in + 1]
    y = jnp.dot(a_refs[0][...], w_refs[0][...], preferred_element_type=F32)
    for a_ref, w_ref in zip(a_refs[1:], w_refs[1:]):
        y = y + jnp.dot(a_ref[...], w_ref[...], preferred_element_type=F32)
    o_ref[...] = x_ref[...] + y


def proj_residual(acts, w, layer, x, tm=1024, tn=512):
    m, n = x.shape
    n_in = len(acts)
    k = acts[0].shape[1]
    in_specs = [pl.BlockSpec((tm, k), lambda i, j: (i, 0)) for _ in acts]
    in_specs += [pl.BlockSpec((None, k, tn), functools.partial(lambda i, j, s: (layer, s, j), s=s))
                 for s in range(n_in)]
    in_specs += [pl.BlockSpec((tm, tn), lambda i, j: (i, j))]
    return pl.pallas_call(
        functools.partial(_proj_residual_kernel, n_in=n_in),
        grid=(m // tm, n // tn),
        in_specs=in_specs,
        out_specs=pl.BlockSpec((tm, tn), lambda i, j: (i, j)),
        out_shape=jax.ShapeDtypeStruct((m, n), F32),
        compiler_params=_params("parallel", "arbitrary"),
        name="proj_residual",
    )(*acts, *([w] * n_in), x)


def _rmsnorm_kernel(x_ref, w_ref, o_ref):
    x = x_ref[...]
    ms = jnp.mean(x * x, axis=-1, keepdims=True)
    o_ref[...] = x * lax.rsqrt(ms + NORM_EPS) * w_ref[...]


def rmsnorm(x, w, tm=512):
    m, d = x.shape
    return pl.pallas_call(
        _rmsnorm_kernel,
        grid=(m // tm,),
        in_specs=[pl.BlockSpec((tm, d), lambda i: (i, 0)), pl.BlockSpec((1, d), lambda i: (0, 0))],
        out_specs=pl.BlockSpec((tm, d), lambda i: (i, 0)),
        out_shape=jax.ShapeDtypeStruct((m, d), F32),
        compiler_params=_params("parallel"),
        name="final_rmsnorm",
    )(x, w.reshape(1, d))


def _split3(x):
    hi = x.astype(BF16)
    r1 = x - hi.astype(F32)
    mid = r1.astype(BF16)
    lo = (r1 - mid.astype(F32)).astype(BF16)
    return hi, mid, lo


def _hgrn2_levels(c_len, sub):
    levels, seg, half = [], 0, sub
    while half < c_len:
        levels.append((half, seg))
        seg += c_len // (2 * half)
        half *= 2
    return levels, seg


def _hgrn2_kernel(lbl_ref, gain_ref, q_ref, f_ref, v_ref, g_ref, o_ref, st_ref, b_ref, k_ref, vc_ref, *,
                  layer, hb, tblk):
    c_len, sub = HG_CHUNK, HG_SUB
    n_sub = c_len // sub
    levels, n_seg = _hgrn2_levels(c_len, sub)
    lg = lbl_ref[...]
    ex = jnp.exp(lg - jnp.max(lg, axis=0, keepdims=True))
    sm = ex / jnp.sum(ex, axis=0, keepdims=True)
    lb_all = sm[0:1]
    for j in range(1, layer + 1):
        lb_all = lb_all + sm[j:j + 1]
    lb_all = lb_all - sm[0:1]
    gain_all = gain_ref[...]

    @pl.when(pl.program_id(2) == 0)
    def _():
        st_ref[...] = jnp.zeros_like(st_ref)

    tri = jnp.where(lax.broadcasted_iota(jnp.int32, (c_len, c_len), 0)
                    >= lax.broadcasted_iota(jnp.int32, (c_len, c_len), 1), 1.0, 0.0).astype(BF16)
    ones_kk = jnp.ones((A_DK, A_DK), BF16)
    row_sub = lax.broadcasted_iota(jnp.int32, (sub, A_DK), 0)
    zero_blk = jnp.zeros((sub, A_DK), F32)

    def head_chunk(h, r0):
        rows = pl.ds(r0, c_len)
        cols = slice(h * A_DK, (h + 1) * A_DK)
        lb = lb_all[:, cols]
        one_m_lb = 1.0 - lb
        q = _silu(q_ref[rows, cols])
        fz = f_ref[rows, cols]
        v = v_ref[rows, cols]
        vc_ref[h] = v
        v_b = v.astype(BF16)
        f = lb + one_m_lb * jax.nn.sigmoid(fz)
        hi, mid, lo = _split3(jnp.log2(jnp.maximum(f, TINY)))
        kk = one_m_lb * jax.nn.sigmoid(-fz)
        b = (jnp.dot(tri, hi, preferred_element_type=F32)
             + jnp.dot(tri, mid, preferred_element_type=F32)
             + jnp.dot(tri, lo, preferred_element_type=F32))
        b_ref[h] = b
        k_ref[h] = kk
        st = st_ref[h]
        o_inter = lax.dot_general((q * jnp.exp2(b)).astype(BF16), st.astype(BF16), _NT,
                                  preferred_element_type=F32)

        es = []
        for j in range(n_sub):
            bj = b[j * sub:(j + 1) * sub]
            qj = q[j * sub:(j + 1) * sub]
            for s in range(sub):
                bs = b_ref[h, pl.ds(j * sub + s, 1), :]
                ks = k_ref[h, pl.ds(j * sub + s, 1), :]
                es.append(qj * jnp.exp2(jnp.where(row_sub >= s, bj - bs, NEG_INF)) * ks)
        att_d = jnp.dot(jnp.concatenate(es, axis=0).astype(BF16), ones_kk, preferred_element_type=F32)
        o_diag = []
        for j in range(n_sub):
            acc = None
            for s in range(sub):
                u = j * sub + s
                term = att_d[u * sub:(u + 1) * sub] * vc_ref[h, pl.ds(u, 1), :]
                acc = term if acc is None else acc + term
            o_diag.append(acc)

        q_rows, k_rows = [], []
        for j in range(n_sub):
            bj = b[j * sub:(j + 1) * sub]
            q_seg = [zero_blk] * n_seg
            k_seg = [zero_blk] * n_seg
            for half, seg0 in levels:
                blk = (j * sub) // (2 * half)
                m_row = b_ref[h, pl.ds(blk * 2 * half + half - 1, 1), :]
                if ((j * sub) // half) % 2 == 1:
                    q_seg[seg0 + blk] = q[j * sub:(j + 1) * sub] * jnp.exp2(bj - m_row)
                else:
                    k_seg[seg0 + blk] = kk[j * sub:(j + 1) * sub] * jnp.exp2(m_row - bj)
            q_rows.append(jnp.concatenate(q_seg, axis=1))
            k_rows.append(jnp.concatenate(k_seg, axis=1))
        att_o = lax.dot_general(jnp.concatenate(q_rows, axis=0).astype(BF16),
                                jnp.concatenate(k_rows, axis=0).astype(BF16), _NT,
                                preferred_element_type=F32)
        o = (o_inter + jnp.dot(att_o.astype(BF16), v_b, preferred_element_type=F32)
             + jnp.concatenate(o_diag, axis=0))

        b_last = b_ref[h, pl.ds(c_len - 1, 1), :]
        k_dec = (kk * jnp.exp2(b_last - b)).astype(BF16)
        st_ref[h] = st * jnp.exp2(b_last) + lax.dot_general(v_b, k_dec, _TN, preferred_element_type=F32)

        o = o * lax.rsqrt(jnp.mean(o * o, axis=-1, keepdims=True) + NORM_EPS) * gain_all[:, cols]
        o_ref[rows, cols] = (o * _silu(g_ref[rows, cols])).astype(o_ref.dtype)

    def chunk(c, carry):
        r0 = pl.multiple_of(c * c_len, c_len)
        for h in range(hb):
            head_chunk(h, r0)
        return carry

    lax.fori_loop(0, tblk // c_len, chunk, 0)


def hgrn2(p, lb_logits, gain, layer, bsz, seq, col_q, col_f, col_v, col_g, hb=8, tblk=256):
    n_layers = lb_logits.shape[0]
    nt = seq // tblk
    w = hb * A_DK
    blk = lambda c0: pl.BlockSpec((tblk, w), lambda b, h, t: (b * nt + t, c0 // hb + h))
    return pl.pallas_call(
        functools.partial(_hgrn2_kernel, layer=layer, hb=hb, tblk=tblk),
        grid=(bsz, A_HEADS // hb, nt),
        in_specs=[
            pl.BlockSpec((n_layers, w), lambda b, h, t: (0, h)),
            pl.BlockSpec((1, w), lambda b, h, t: (0, h)),
            blk(col_q), blk(col_f), blk(col_v), blk(col_g),
        ],
        out_specs=pl.BlockSpec((tblk, w), lambda b, h, t: (b * nt + t, h)),
        out_shape=jax.ShapeDtypeStruct((bsz * seq, A_HEADS * A_DV), BF16),
        scratch_shapes=[pltpu.VMEM((hb, A_DV, A_DK), F32), pltpu.VMEM((hb, HG_CHUNK, A_DK), F32),
                        pltpu.VMEM((hb, HG_CHUNK, A_DK), F32), pltpu.VMEM((hb, HG_CHUNK, A_DV), F32)],
        compiler_params=_params("parallel", "parallel", "arbitrary"),
        name="hgrn2",
    )(lb_logits, gain.reshape(1, -1), p, p, p, p)


def _position_features(pos, lane):
    ab = jnp.where((lane & 1) == 0, lax.shift_right_logical(pos, 6), pos & 63)
    return jnp.where(lane >= LANE_ALIBI, jnp.where(lane < LANE_PAD, ab, 0), 0)


def _compress_block(a_ref, pe_ref, w1_ref, w2_ref, n_units, transposed=False):
    h_a = jnp.zeros((n_units, B_DH), F32)
    h_b = jnp.zeros((n_units, B_DH), F32)
    for r in range(CMP_STRIDE):
        a_r = a_ref[pl.ds(r, n_units, stride=CMP_STRIDE), :]
        h_a = h_a + jnp.dot((a_r + pe_ref[r:r + 1, :]).astype(BF16), w1_ref[r * B_DH:(r + 1) * B_DH, :],
                            preferred_element_type=F32)
        r2 = CMP_STRIDE + r
        h_b = h_b + jnp.dot((a_r + pe_ref[r2:r2 + 1, :]).astype(BF16), w1_ref[r2 * B_DH:(r2 + 1) * B_DH, :],
                            preferred_element_type=F32)
    hid = _silu(h_a + pltpu.roll(h_b, n_units - 1, 0)).astype(BF16)
    if transposed:
        return lax.dot_general(w2_ref[...], hid, _NT, preferred_element_type=F32)
    return jnp.dot(hid, w2_ref[...], preferred_element_type=F32)


def _nsa_kernel(feat_ref, q_ref, kcr_ref, vcr_ref, ks_ref, kw_ref, vst_ref, vwt_ref, bg_ref, gate_ref,
                pek_ref, w1k_ref, w2k_ref, pev_ref, w1v_ref, w2vt_ref,
                o_ref, ksa_ref, kwa_ref, vwp_ref, kca_ref, vct_ref, madd_ref, cpat_ref, score_ref, cin_ref, *,
                seq, gb):
    tq, kb_len, hpg, dh = NSA_TQ, NSA_KB, B_HPG, B_DH
    rows_all = hpg * tq
    n_slc = seq // SEL_LEN
    n_units = seq // CMP_STRIDE
    n_cmp = (seq - CMP_LEN) // CMP_STRIDE + 1
    qi = pl.program_id(2)
    t0 = pl.multiple_of(qi * tq, tq)

    @pl.when(qi == 0)
    def _():
        pos = lax.broadcasted_iota(jnp.int32, (seq, dh), 0)
        lane = lax.broadcasted_iota(jnp.int32, (seq, dh), 1)
        alibi = _position_features(pos, lane)
        blk_hot = jnp.where(lane == lax.shift_right_logical(pos, 6), 1, 0)
        chunk_hot = jnp.where(lane - LANE_CHUNK == lax.shift_right_logical(pos, 7), 1, 0)
        feat_s = (alibi + blk_hot + chunk_hot).astype(F32).astype(BF16)
        feat_w = alibi.astype(F32).astype(BF16)
        lane_w = lax.broadcasted_iota(jnp.int32, (WIN, dh), 1)
        pad_w = jnp.where(lane_w == LANE_PAD, 1.0, 0.0).astype(BF16)
        c_end = lax.broadcasted_iota(jnp.int32, (n_units, dh), 0) * CMP_STRIDE + (CMP_LEN - 1)
        lane_c = lax.broadcasted_iota(jnp.int32, (n_units, dh), 1)
        feat_c = _position_features(c_end, lane_c).astype(F32).astype(BF16)
        for gi in range(gb):
            gc = slice(gi * dh, (gi + 1) * dh)
            ksa_ref[gi, :, 0:dh] = ks_ref[:, gc].astype(BF16)
            ksa_ref[gi, :, dh:2 * dh] = feat_s
            kwa_ref[gi, WIN:WIN + seq, 0:dh] = kw_ref[:, gc].astype(BF16)
            kwa_ref[gi, WIN:WIN + seq, dh:2 * dh] = feat_w
            kwa_ref[gi, 0:WIN, 0:dh] = jnp.zeros((WIN, dh), BF16)
            kwa_ref[gi, 0:WIN, dh:2 * dh] = pad_w
            vwp_ref[gi, :, 0:WIN] = jnp.zeros((dh, WIN), BF16)
            vwp_ref[gi, :, WIN:WIN + seq] = vwt_ref[gc, :]
            cin_ref[...] = kcr_ref[:, gc]
            kca_ref[gi, :, 0:dh] = _compress_block(cin_ref, pek_ref, w1k_ref, w2k_ref, n_units).astype(BF16)
            kca_ref[gi, :, dh:2 * dh] = feat_c
            cin_ref[...] = vcr_ref[:, gc]
            vct_ref[gi] = _compress_block(cin_ref, pev_ref, w1v_ref, w2vt_ref, n_units,
                                          transposed=True).astype(BF16)
        j_loc = lax.broadcasted_iota(jnp.int32, (tq, rows_all), 0)
        t_loc = lax.broadcasted_iota(jnp.int32, (tq, rows_all), 1) & (tq - 1)
        madd_ref[0] = jnp.where(j_loc > t_loc, 0.0, -MASK_BIG)
        madd_ref[1] = jnp.where(j_loc <= t_loc, 0.0, -MASK_BIG)
        cpat_ref[...] = ((lax.broadcasted_iota(jnp.int32, (n_units, rows_all), 1) & (tq - 1))
                         - lax.broadcasted_iota(jnp.int32, (n_units, rows_all), 0) * CMP_STRIDE)

    def stack_heads(q2t, extras):
        return jnp.concatenate(
            [jnp.concatenate([q2t[h], extras[h].astype(BF16)], axis=0) for h in range(hpg)], axis=1)

    def softmax_cols(s):
        e = jnp.exp2(s - jnp.max(s, axis=0, keepdims=True))
        return e.astype(BF16), jnp.sum(e, axis=0, keepdims=True)

    row1 = lax.broadcasted_iota(jnp.int32, (dh, 1), 0)
    row_q = lax.broadcasted_iota(jnp.int32, (dh, tq), 0)
    span = WIN + tq
    gw = hpg * dh

    def before_loop(gi):
        qf = q_ref[:, gi * gw:(gi + 1) * gw]
        q2t = [jnp.transpose(qf[:, h * dh:(h + 1) * dh] * (dh ** -0.5 * LOG2E)).astype(BF16) for h in range(hpg)]
        feat = feat_ref[gi]
        q_w = stack_heads(q2t, [jnp.broadcast_to(feat[:, h:h + 1], (dh, tq)) for h in range(hpg)])

        s_w = jnp.dot(kwa_ref[gi, pl.ds(t0, span), :], q_w, preferred_element_type=F32)
        s_w = jnp.concatenate([s_w[0:tq] + madd_ref[0], s_w[tq:WIN], s_w[WIN:span] + madd_ref[1]], axis=0)
        e_w, l_w = softmax_cols(s_w)
        o_win = jnp.dot(vwp_ref[gi, :, pl.ds(t0, span)], e_w, preferred_element_type=F32) * (1.0 / l_w)

        mask_c = cpat_ref[...] >= (CMP_LEN - 1) - t0
        s_c = jnp.where(mask_c, jnp.dot(kca_ref[gi], q_w, preferred_element_type=F32), NEG_INF)
        e_c = jnp.where(mask_c, jnp.exp2(s_c - jnp.max(s_c, axis=0, keepdims=True)), 0.0)
        l_c = jnp.sum(e_c, axis=0, keepdims=True)
        p_c = e_c * (1.0 / jnp.where(l_c > 0.0, l_c, 1.0))
        o_cmp = jnp.dot(vct_ref[gi], p_c.astype(BF16), preferred_element_type=F32)

        p_sum = p_c[:, 0:tq]
        for h in range(1, hpg):
            p_sum = p_sum + p_c[:, h * tq:(h + 1) * tq]
        jn = lax.broadcasted_iota(jnp.int32, (n_slc, n_units), 0) * SEL_LEN
        cn = lax.broadcasted_iota(jnp.int32, (n_slc, n_units), 1) * CMP_STRIDE
        ov_t = jnp.where(cn <= jn + (SEL_LEN - 1),
                         jnp.where(cn + (CMP_LEN - 1) >= jn, jnp.where(cn < n_cmp * CMP_STRIDE, 1.0, 0.0), 0.0),
                         0.0).astype(BF16)
        p_hi = p_sum.astype(BF16)
        p_lo = (p_sum - p_hi.astype(F32)).astype(BF16)
        imp_t = (jnp.dot(ov_t, p_hi, preferred_element_type=F32)
                 + jnp.dot(ov_t, p_lo, preferred_element_type=F32))
        blk = lax.broadcasted_iota(jnp.int32, (n_slc, tq), 0)
        cur = lax.shift_right_logical(t0 + lax.broadcasted_iota(jnp.int32, (n_slc, tq), 1), 6)
        forced = jnp.where(blk == 0, 1, jnp.where(blk == cur, 1, jnp.where(blk == cur - 1, 1, 0)))
        score = jnp.where(forced > 0, FORCE_SCORE, jnp.where(blk <= cur, imp_t, NEG_INF))
        score_ref[gi] = score
        rank = jnp.zeros((n_slc, tq), jnp.int32)
        for j in range(n_slc):
            sj = score_ref[gi, pl.ds(j, 1), :]
            rank = rank + jnp.where(sj > score, 1, jnp.where(blk > j, jnp.where(sj == score, 1, 0), 0))
        sel_bias_t = jnp.where(rank < SEL_TOPK, 0.0, -MASK_BIG)
        sel_bias = jnp.concatenate([sel_bias_t, jnp.zeros((dh - n_slc, tq), F32)], axis=0)

        feat_np = jnp.where(row1 == LANE_PAD, 0.0, feat)
        chunk_col = jnp.where(row1 >= LANE_CHUNK,
                              jnp.where(row1 < LANE_CHUNK + seq // tq,
                                        jnp.where(row1 - LANE_CHUNK < qi, 0.0, -MASK_BIG), 0.0), 0.0)
        extra_d = [jnp.where(row_q < n_slc, sel_bias, feat_np[:, h:h + 1]) for h in range(hpg)]
        q_d = stack_heads(q2t, extra_d)
        q_s = stack_heads(q2t, [x + chunk_col for x in extra_d])
        s_d = jnp.dot(ksa_ref[gi, pl.ds(t0, tq), :], q_d, preferred_element_type=F32) + madd_ref[1]
        m0 = jnp.max(s_d, axis=0, keepdims=True)
        e_d = jnp.exp2(s_d - m0)
        l0 = jnp.sum(e_d, axis=0, keepdims=True)
        acc0 = jnp.dot(vst_ref[gi * dh:(gi + 1) * dh, pl.ds(t0, tq)], e_d.astype(BF16),
                       preferred_element_type=F32)
        return o_win, o_cmp, q_s, (m0, l0, acc0)

    pre = [before_loop(gi) for gi in range(gb)]

    def sel_step(kb, carry):
        k0 = pl.multiple_of(kb * kb_len, kb_len)
        out = []
        for gi in range(gb):
            m, l, acc = carry[gi]
            s = jnp.dot(ksa_ref[gi, pl.ds(k0, kb_len), :], pre[gi][2], preferred_element_type=F32)
            m_new = jnp.maximum(m, jnp.max(s, axis=0, keepdims=True))
            alpha = jnp.exp2(m - m_new)
            e = jnp.exp2(s - m_new)
            l = alpha * l + jnp.sum(e, axis=0, keepdims=True)
            acc = alpha * acc + jnp.dot(vst_ref[gi * dh:(gi + 1) * dh, pl.ds(k0, kb_len)], e.astype(BF16),
                                        preferred_element_type=F32)
            out.append((m_new, l, acc))
        return tuple(out)

    n_kb = lax.shift_right_logical(t0 + kb_len - 1, int(math.log2(kb_len)))
    sel = lax.fori_loop(0, n_kb, sel_step, tuple(p[3] for p in pre))

    for gi in range(gb):
        o_win, o_cmp = pre[gi][0], pre[gi][1]
        _, l_s, acc_s = sel[gi]
        o_sel = acc_s * (1.0 / l_s)
        sg_t = jnp.transpose(jax.nn.sigmoid(gate_ref[:, gi * dh:(gi + 1) * dh]))
        grow = lambda br: jnp.concatenate([sg_t[3 * h + br:3 * h + br + 1, :] for h in range(hpg)], axis=1)
        o = grow(0) * o_cmp + grow(1) * o_sel + grow(2) * o_win
        for h in range(hpg):
            c0 = gi * gw + h * dh
            o_h = jnp.transpose(o[:, h * tq:(h + 1) * tq])
            o_ref[:, c0:c0 + dh] = (o_h * _silu(bg_ref[:, c0:c0 + dh])).astype(o_ref.dtype)


def _slope_features():
    slopes = jnp.asarray(2.0 ** (-8.0 * np.arange(1, B_HEADS + 1) / B_HEADS) * LOG2E, dtype=F32)
    hi = slopes.astype(BF16).astype(F32)
    mid = (slopes - hi).astype(BF16).astype(F32)
    lo = (slopes - hi - mid).astype(BF16).astype(F32)
    cols = jnp.stack([64.0 * hi, hi, 64.0 * mid, mid, 64.0 * lo, lo, jnp.full_like(hi, -MASK_BIG)], axis=1)
    feat = jnp.zeros((B_HEADS, B_DH), F32).at[:, LANE_ALIBI:LANE_PAD + 1].set(cols)
    feat = feat.reshape(B_KV, B_HPG, B_DH)
    return jnp.swapaxes(jnp.pad(feat, ((0, 0), (0, 8 - B_HPG), (0, 0))), 1, 2)


def nsa(p, pt, pvt, pe_k, w1_k, w2_k, pe_v, w1_v, w2t_v, layer, bsz, seq,
        col_q, col_kc, col_vc, col_ks, col_kw, col_bg, col_gate):
    nq = seq // NSA_TQ
    gb = NSA_GB
    gw = B_HPG * B_DH
    n_units = seq // CMP_STRIDE
    rows_all = B_HPG * NSA_TQ
    assert all(c % gb == 0 for c in (col_q, col_kc, col_vc, col_ks, col_kw, col_bg, col_gate)) and B_KV % gb == 0
    tile = lambda c0, w: pl.BlockSpec((NSA_TQ, gb * w), lambda b, g, i: (b * nq + i, c0 // gb + g))
    full = lambda c0: pl.BlockSpec((seq, gb * B_DH), lambda b, g, i: (b, c0 // gb + g))
    full_t = lambda r0: pl.BlockSpec((gb * B_DH, seq), lambda b, g, i: (r0 // gb + g, b))
    wspec = lambda a: pl.BlockSpec((None,) + a.shape[1:], lambda b, g, i: (layer,) + (0,) * (a.ndim - 1))
    return pl.pallas_call(
        functools.partial(_nsa_kernel, seq=seq, gb=gb),
        grid=(bsz, B_KV // gb, nq),
        in_specs=[
            pl.BlockSpec((gb, B_DH, 8), lambda b, g, i: (g, 0, 0)),
            tile(col_q, gw), full(col_kc), full(col_vc), full(col_ks), full(col_kw),
            full_t(0), full_t(B_KV),
            tile(col_bg, gw), tile(col_gate, B_DH),
            wspec(pe_k), wspec(w1_k), wspec(w2_k), wspec(pe_v), wspec(w1_v), wspec(w2t_v),
        ],
        out_specs=pl.BlockSpec((NSA_TQ, gb * gw), lambda b, g, i: (b * nq + i, g)),
        out_shape=jax.ShapeDtypeStruct((bsz * seq, B_HEADS * B_DH), BF16),
        scratch_shapes=[
            pltpu.VMEM((gb, seq, 2 * B_DH), BF16),
            pltpu.VMEM((gb, seq + WIN, 2 * B_DH), BF16),
            pltpu.VMEM((gb, B_DH, seq + WIN), BF16),
            pltpu.VMEM((gb, n_units, 2 * B_DH), BF16),
            pltpu.VMEM((gb, B_DH, n_units), BF16),
            pltpu.VMEM((2, NSA_TQ, rows_all), F32),
            pltpu.VMEM((n_units, rows_all), jnp.int32),
            pltpu.VMEM((gb, seq // SEL_LEN, NSA_TQ), F32),
            pltpu.VMEM((seq, B_DH), F32),
        ],
        compiler_params=_params("parallel", "parallel", "arbitrary"),
        name="nsa_attention",
    )(_slope_features(), p, p, p, p, p, pvt, pvt, pt, pt, pe_k, w1_k, w2_k, pe_v, w1_v, w2t_v)


def _softplus(x):
    return jnp.maximum(x, 0.0) + jnp.log1p(jnp.exp(-jnp.abs(x)))


def _rglru_kernel(xb_ref, g_ref, cw_ref, cb_ref, wa_ref, ba_ref, wi_ref, bi_ref, lam_ref, o_ref,
                  xpad_ref, h_ref):
    tb = RG_TB
    pad = 8

    @pl.when(pl.program_id(2) == 0)
    def _():
        xpad_ref[0:pad, :] = jnp.zeros((pad, RG_BS), F32)
        h_ref[...] = jnp.zeros_like(h_ref)

    x = xb_ref[...]
    xpad_ref[pad:pad + tb, :] = x
    xc = xpad_ref[pl.ds(pad - 3, tb), :] * cw_ref[0:1, :]
    for j in range(1, CONV_W):
        xc = xc + xpad_ref[pl.ds(pad - 3 + j, tb), :] * cw_ref[j:j + 1, :]
    xc = xc + cb_ref[...]
    xpad_ref[0:pad, :] = x[tb - pad:tb]

    xc_b = xc.astype(BF16)
    r = jax.nn.sigmoid(jnp.dot(xc_b, wa_ref[...], preferred_element_type=F32) + ba_ref[...])
    i = jax.nn.sigmoid(jnp.dot(xc_b, wi_ref[...], preferred_element_type=F32) + bi_ref[...])
    log_a = -RG_C * _softplus(-lam_ref[...]) * r
    a = jnp.exp(log_a)
    th = jnp.tanh(log_a)
    u = jnp.sqrt(jnp.maximum(-2.0 * th / (1.0 - th), 0.0)) * (i * xc)

    row = lax.broadcasted_iota(jnp.int32, (tb, RG_BS), 0)
    sh = 1
    while sh < tb:
        keep = row >= sh
        a_s = jnp.where(keep, pltpu.roll(a, sh, 0), 1.0)
        u_s = jnp.where(keep, pltpu.roll(u, sh, 0), 0.0)
        u = a * u_s + u
        a = a * a_s
        sh *= 2
    h = u + a * h_ref[...]
    h_ref[...] = h[tb - 1:tb]
    o_ref[...] = (h * _silu(g_ref[...])).astype(o_ref.dtype)


def rglru(p, conv_w, conv_b, w_a, b_a, w_i, b_i, lam, layer, bsz, seq):
    nt = seq // RG_TB
    d_rnn = RG_BLOCKS * RG_BS
    tile = lambda c0: pl.BlockSpec((RG_TB, RG_BS), lambda b, n, t: (b * nt + t, c0 + n))
    vec = lambda rows: pl.BlockSpec((None, rows, RG_BS), lambda b, n, t: (layer, 0, n))
    mat = pl.BlockSpec((None, None, RG_BS, RG_BS), lambda b, n, t: (layer, n, 0, 0))
    n_layers = conv_b.shape[0]
    row = lambda a: a.reshape(n_layers, 1, d_rnn)
    return pl.pallas_call(
        _rglru_kernel,
        grid=(bsz, RG_BLOCKS, nt),
        in_specs=[tile(0), tile(RG_BLOCKS), vec(CONV_W), vec(1), mat, vec(1), mat, vec(1), vec(1)],
        out_specs=pl.BlockSpec((RG_TB, RG_BS), lambda b, n, t: (b * nt + t, n)),
        out_shape=jax.ShapeDtypeStruct((bsz * seq, d_rnn), BF16),
        scratch_shapes=[pltpu.VMEM((RG_TB + 8, RG_BS), F32), pltpu.VMEM((1, RG_BS), F32)],
        compiler_params=_params("parallel", "parallel", "arbitrary"),
        name="rglru",
    )(p, p, conv_w, row(conv_b), w_a, row(b_a), w_i, row(b_i), row(lam))


def _even_tail_weight(w, sizes):
    start = int(np.sum(sizes[:-2]))
    n_gate, d_b = sizes[-2], sizes[-1]
    gate = w[:, :, start:start + n_gate].reshape(w.shape[0], w.shape[1], B_KV, 3 * B_HPG)
    gate = jnp.pad(gate, ((0, 0), (0, 0), (0, 0), (0, B_DH - 3 * B_HPG))).reshape(w.shape[0], w.shape[1], B_KV * B_DH)
    return jnp.concatenate([w[:, :, start + n_gate:start + n_gate + d_b], gate], axis=2).astype(BF16)


def kernel(x, norm_w, final_norm_w, even_w_in, even_w_out, hgrn_lb_logits, hgrn_norm_w, cmp_pe_k, cmp_w1_k, cmp_w2_k, cmp_pe_v, cmp_w1_v, cmp_w2_v, odd_w_in, odd_w_out, rg_conv_w, rg_conv_b, rg_w_a, rg_b_a, rg_w_i, rg_b_i, rg_lambda):
    bsz, seq, d_model = x.shape
    depth = norm_w.shape[0]
    d_a = A_HEADS * A_DV
    d_b = B_HEADS * B_DH
    d_rnn = RG_BLOCKS * RG_BS
    a_qk = A_HEADS * A_DK
    kvw = B_KV * B_DH
    gw = B_HPG * B_DH
    assert seq % NSA_KB == 0 and seq % RG_TB == 0 and seq % HG_CHUNK == 0 and CMP_LEN == 2 * CMP_STRIDE
    assert seq // SEL_LEN <= LANE_ALIBI and seq // NSA_TQ <= B_DH - LANE_CHUNK and seq >= WIN + NSA_TQ
    sizes = (a_qk, a_qk, d_a, d_a, d_b, kvw, kvw, kvw, kvw, kvw, kvw, 3 * B_HEADS, d_b)
    names = ("a_q", "a_f", "a_i", "a_g", "b_q", "b_kc", "b_vc", "b_ks", "b_vs", "b_kw", "b_vw", "b_gate", "b_g")
    off = dict(zip(names, np.concatenate([[0], np.cumsum(sizes)[:-1]]).tolist()))
    tn = 512
    assert all(off[k] % tn == 0 for k in ("b_vs", "b_kw", "b_vw")) and kvw == tn
    skip_tile = off["b_vs"] // tn
    n_main_tiles = off["b_vw"] // tn - 1
    col_kw = (off["b_kw"] - tn) // B_DH

    even_in_b = even_w_in.astype(BF16)
    even_tail_b = _even_tail_weight(even_w_in, sizes)
    vcols = lambda name: jnp.swapaxes(even_w_in[:, :, off[name]:off[name] + kvw], 1, 2)
    even_vt_b = jnp.concatenate([vcols("b_vs"), vcols("b_vw")], axis=1).astype(BF16)
    even_out_b = even_w_out.astype(BF16)
    odd_in_b = odd_w_in.astype(BF16)
    odd_out_b = odd_w_out.astype(BF16)
    w_a_b, w_i_b = rg_w_a.astype(BF16), rg_w_i.astype(BF16)
    w1k_b, w2k_b = cmp_w1_k.astype(BF16), cmp_w2_k.astype(BF16)
    w1v_b, w2tv_b = cmp_w1_v.astype(BF16), jnp.swapaxes(cmp_w2_v, 1, 2).astype(BF16)

    xf = x.reshape(bsz * seq, d_model)
    for layer in range(depth):
        if layer % 2 == 0:
            e = layer // 2
            p = norm_matmul(xf, norm_w[layer], even_in_b, e, n_main_tiles, skip_tile)
            pt = norm_matmul(xf, norm_w[layer], even_tail_b, e, even_tail_b.shape[2] // tn)
            pvt = norm_matmul_t(xf, norm_w[layer], even_vt_b, e)
            ya = hgrn2(p, hgrn_lb_logits, hgrn_norm_w[e], e, bsz, seq,
                       off["a_q"] // A_DK, off["a_f"] // A_DK, off["a_i"] // A_DV, off["a_g"] // A_DV)
            yb = nsa(p, pt, pvt, cmp_pe_k, w1k_b, w2k_b, cmp_pe_v, w1v_b, w2tv_b, e, bsz, seq,
                     off["b_q"] // gw, off["b_kc"] // B_DH, off["b_vc"] // B_DH, off["b_ks"] // B_DH,
                     col_kw, 0, d_b // B_DH)
            xf = proj_residual([ya, yb], even_out_b, e, xf)
        else:
            o = layer // 2
            p = norm_matmul(xf, norm_w[layer], odd_in_b, o, 2 * d_rnn // tn)
            hr = rglru(p, rg_conv_w, rg_conv_b, w_a_b, rg_b_a, w_i_b, rg_b_i, rg_lambda, o, bsz, seq)
            xf = proj_residual([hr], odd_out_b, o, xf)
    return rmsnorm(xf, final_norm_w).reshape(bsz, seq, d_model)
```

```python
import functools
import math

import jax
import jax.numpy as jnp
import numpy as np
from jax import lax
from jax.experimental import pallas as pl
from jax.experimental.pallas import tpu as pltpu

F32 = jnp.float32
BF16 = jnp.bfloat16

NORM_EPS = 1e-6
NEG_INF = -1e30
FORCE_SCORE = 1e30
TINY = 1e-30
LOG2E = 1.4426950408889634

A_HEADS = 16
A_DK = 128
A_DV = 128
HG_CHUNK = 64
HG_SUB = 8

B_HEADS = 16
B_DH = 128
B_KV = 4
B_HPG = B_HEADS // B_KV
CMP_LEN = 32
CMP_STRIDE = 16
SEL_LEN = 64
SEL_TOPK = 16
WIN = 512
NSA_TQ = 128
NSA_KB = 512
NSA_GB = 2
MASK_BIG = 32768.0

LANE_ALIBI = 32
LANE_PAD = 38
LANE_CHUNK = 40

RG_BLOCKS = 10
RG_BS = 256
CONV_W = 4
RG_C = 8.0
RG_TB = 256

VMEM_LIMIT = 48 * 1024 * 1024

_NT = (((1,), (1,)), ((), ()))
_TN = (((0,), (0,)), ((), ()))


def _silu(x):
    return x * jax.nn.sigmoid(x)


def _params(*sem):
    return pltpu.CompilerParams(dimension_semantics=sem, vmem_limit_bytes=VMEM_LIMIT)


def _norm_matmul_kernel(x_ref, nw_ref, w_ref, o_ref, h_ref):
    @pl.when(pl.program_id(1) == 0)
    def _():
        x = x_ref[...]
        ms = jnp.mean(x * x, axis=-1, keepdims=True)
        h_ref[...] = (x * lax.rsqrt(ms + NORM_EPS) * nw_ref[...]).astype(BF16)

    o_ref[...] = jnp.dot(h_ref[...], w_ref[...], preferred_element_type=F32)


def norm_matmul(x, nw, w, layer, n_tiles, skip_tile=None, tm=1024, tn=512):
    m, d = x.shape
    src = (lambda j: j) if skip_tile is None else (lambda j: j + j // skip_tile)
    assert skip_tile is None or n_tiles <= 2 * skip_tile
    return pl.pallas_call(
        _norm_matmul_kernel,
        grid=(m // tm, n_tiles),
        in_specs=[
            pl.BlockSpec((tm, d), lambda i, j: (i, 0)),
            pl.BlockSpec((1, d), lambda i, j: (0, 0)),
            pl.BlockSpec((None, d, tn), lambda i, j: (layer, 0, src(j))),
        ],
        out_specs=pl.BlockSpec((tm, tn), lambda i, j: (i, j)),
        out_shape=jax.ShapeDtypeStruct((m, n_tiles * tn), F32),
        scratch_shapes=[pltpu.VMEM((tm, d), BF16)],
        compiler_params=_params("parallel", "arbitrary"),
        name="norm_matmul",
    )(x, nw.reshape(1, d), w)


def _norm_matmul_t_kernel(x_ref, nw_ref, wt_ref, o_ref, h_ref):
    @pl.when(pl.program_id(1) == 0)
    def _():
        x = x_ref[...]
        ms = jnp.mean(x * x, axis=-1, keepdims=True)
        h_ref[...] = (x * lax.rsqrt(ms + NORM_EPS) * nw_ref[...]).astype(BF16)

    o_ref[...] = lax.dot_general(wt_ref[...], h_ref[...], _NT, preferred_element_type=F32).astype(o_ref.dtype)


def norm_matmul_t(x, nw, wt, layer, tm=1024, tn=512):
    m, d = x.shape
    n = wt.shape[1]
    return pl.pallas_call(
        _norm_matmul_t_kernel,
        grid=(m // tm, n // tn),
        in_specs=[
            pl.BlockSpec((tm, d), lambda i, j: (i, 0)),
            pl.BlockSpec((1, d), lambda i, j: (0, 0)),
            pl.BlockSpec((None, tn, d), lambda i, j: (layer, j, 0)),
        ],
        out_specs=pl.BlockSpec((tn, tm), lambda i, j: (j, i)),
        out_shape=jax.ShapeDtypeStruct((n, m), BF16),
        scratch_shapes=[pltpu.VMEM((tm, d), BF16)],
        compiler_params=_params("parallel", "arbitrary"),
        name="norm_matmul_t",
    )(x, nw.reshape(1, d), wt)


def _proj_residual_kernel(*refs, n_in):
    a_refs, w_refs = refs[:n_in], refs[n_in:2 * n_in]
    x_ref, o_ref = refs[2 * n_in], refs[2 * n_in + 1]
    y = jnp.dot(a_refs[0][...], w_refs[0][...], preferred_element_type=F32)
    for a_ref, w_ref in zip(a_refs[1:], w_refs[1:]):
        y = y + jnp.dot(a_ref[...], w_ref[...], preferred_element_type=F32)
    o_ref[...] = x_ref[...] + y


def proj_residual(acts, w, layer, x, tm=1024, tn=512):
    m, n = x.shape
    n_in = len(acts)
    k = acts[0].shape[1]
    in_specs = [pl.BlockSpec((tm, k), lambda i, j: (i, 0)) for _ in acts]
    in_specs += [pl.BlockSpec((None, k, tn), functools.partial(lambda i, j, s: (layer, s, j), s=s))
                 for s in range(n_in)]
    in_specs += [pl.BlockSpec((tm, tn), lambda i, j: (i, j))]
    return pl.pallas_call(
        functools.partial(_proj_residual_kernel, n_in=n_in),
        grid=(m // tm, n // tn),
        in_specs=in_specs,
        out_specs=pl.BlockSpec((tm, tn), lambda i, j: (i, j)),
        out_shape=jax.ShapeDtypeStruct((m, n), F32),
        compiler_params=_params("parallel", "arbitrary"),
        name="proj_residual",
    )(*acts, *([w] * n_in), x)


def _rmsnorm_kernel(x_ref, w_ref, o_ref):
    x = x_ref[...]
    ms = jnp.mean(x * x, axis=-1, keepdims=True)
    o_ref[...] = x * lax.rsqrt(ms + NORM_EPS) * w_ref[...]


def rmsnorm(x, w, tm=512):
    m, d = x.shape
    return pl.pallas_call(
        _rmsnorm_kernel,
        grid=(m // tm,),
        in_specs=[pl.BlockSpec((tm, d), lambda i: (i, 0)), pl.BlockSpec((1, d), lambda i: (0, 0))],
        out_specs=pl.BlockSpec((tm, d), lambda i: (i, 0)),
        out_shape=jax.ShapeDtypeStruct((m, d), F32),
        compiler_params=_params("parallel"),
        name="final_rmsnorm",
    )(x, w.reshape(1, d))


def _split3(x):
    hi = x.astype(BF16)
    r1 = x - hi.astype(F32)
    mid = r1.astype(BF16)
    lo = (r1 - mid.astype(F32)).astype(BF16)
    return hi, mid, lo


def _hgrn2_levels(c_len, sub):
    levels, seg, half = [], 0, sub
    while half < c_len:
        levels.append((half, seg))
        seg += c_len // (2 * half)
        half *= 2
    return levels, seg


def _hgrn2_kernel(lbl_ref, gain_ref, q_ref, f_ref, v_ref, g_ref, o_ref, st_ref, b_ref, k_ref, vc_ref, *,
                  layer, hb, tblk):
    c_len, sub = HG_CHUNK, HG_SUB
    n_sub = c_len // sub
    levels, n_seg = _hgrn2_levels(c_len, sub)
    lg = lbl_ref[...]
    ex = jnp.exp(lg - jnp.max(lg, axis=0, keepdims=True))
    sm = ex / jnp.sum(ex, axis=0, keepdims=True)
    lb_all = sm[0:1]
    for j in range(1, layer + 1):
        lb_all = lb_all + sm[j:j + 1]
    lb_all = lb_all - sm[0:1]
    gain_all = gain_ref[...]

    @pl.when(pl.program_id(2) == 0)
    def _():
        st_ref[...] = jnp.zeros_like(st_ref)

    tri = jnp.where(lax.broadcasted_iota(jnp.int32, (c_len, c_len), 0)
                    >= lax.broadcasted_iota(jnp.int32, (c_len, c_len), 1), 1.0, 0.0).astype(BF16)
    ones_kk = jnp.ones((A_DK, A_DK), BF16)
    row_sub = lax.broadcasted_iota(jnp.int32, (sub, A_DK), 0)
    zero_blk = jnp.zeros((sub, A_DK), F32)

    def head_chunk(h, r0):
        rows = pl.ds(r0, c_len)
        cols = slice(h * A_DK, (h + 1) * A_DK)
        lb = lb_all[:, cols]
        one_m_lb = 1.0 - lb
        q = _silu(q_ref[rows, cols])
        fz = f_ref[rows, cols]
        v = v_ref[rows, cols]
        vc_ref[h] = v
        v_b = v.astype(BF16)
        f = lb + one_m_lb * jax.nn.sigmoid(fz)
        hi, mid, lo = _split3(jnp.log2(jnp.maximum(f, TINY)))
        kk = one_m_lb * jax.nn.sigmoid(-fz)
        b = (jnp.dot(tri, hi, preferred_element_type=F32)
             + jnp.dot(tri, mid, preferred_element_type=F32)
             + jnp.dot(tri, lo, preferred_element_type=F32))
        b_ref[h] = b
        k_ref[h] = kk
        st = st_ref[h]
        o_inter = lax.dot_general((q * jnp.exp2(b)).astype(BF16), st.astype(BF16), _NT,
                                  preferred_element_type=F32)

        es = []
        for j in range(n_sub):
            bj = b[j * sub:(j + 1) * sub]
            qj = q[j * sub:(j + 1) * sub]
            for s in range(sub):
                bs = b_ref[h, pl.ds(j * sub + s, 1), :]
                ks = k_ref[h, pl.ds(j * sub + s, 1), :]
                es.append(qj * jnp.exp2(jnp.where(row_sub >= s, bj - bs, NEG_INF)) * ks)
        att_d = jnp.dot(jnp.concatenate(es, axis=0).astype(BF16), ones_kk, preferred_element_type=F32)
        o_diag = []
        for j in range(n_sub):
            acc = None
            for s in range(sub):
                u = j * sub + s
                term = att_d[u * sub:(u + 1) * sub] * vc_ref[h, pl.ds(u, 1), :]
                acc = term if acc is None else acc + term
            o_diag.append(acc)

        q_rows, k_rows = [], []
        for j in range(n_sub):
            bj = b[j * sub:(j + 1) * sub]
            q_seg = [zero_blk] * n_seg
            k_seg = [zero_blk] * n_seg
            for half, seg0 in levels:
                blk = (j * sub) // (2 * half)
                m_row = b_ref[h, pl.ds(blk * 2 * half + half - 1, 1), :]
                if ((j * sub) // half) % 2 == 1:
                    q_seg[seg0 + blk] = q[j * sub:(j + 1) * sub] * jnp.exp2(bj - m_row)
                else:
                    k_seg[seg0 + blk] = kk[j * sub:(j + 1) * sub] * jnp.exp2(m_row - bj)
            q_rows.append(jnp.concatenate(q_seg, axis=1))
            k_rows.append(jnp.concatenate(k_seg, axis=1))
        att_o = lax.dot_general(jnp.concatenate(q_rows, axis=0).astype(BF16),
                                jnp.concatenate(k_rows, axis=0).astype(BF16), _NT,
                                preferred_element_type=F32)
        o = (o_inter + jnp.dot(att_o.astype(BF16), v_b, preferred_element_type=F32)
             + jnp.concatenate(o_diag, axis=0))

        b_last = b_ref[h, pl.ds(c_len - 1, 1), :]
        k_dec = (kk * jnp.exp2(b_last - b)).astype(BF16)
        st_ref[h] = st * jnp.exp2(b_last) + lax.dot_general(v_b, k_dec, _TN, preferred_element_type=F32)

        o = o * lax.rsqrt(jnp.mean(o * o, axis=-1, keepdims=True) + NORM_EPS) * gain_all[:, cols]
        o_ref[rows, cols] = (o * _silu(g_ref[rows, cols])).astype(o_ref.dtype)

    def chunk(c, carry):
        r0 = pl.multiple_of(c * c_len, c_len)
        for h in range(hb):
            head_chunk(h, r0)
        return carry

    lax.fori_loop(0, tblk // c_len, chunk, 0)


def hgrn2(p, lb_logits, gain, layer, bsz, seq, col_q, col_f, col_v, col_g, hb=16, tblk=128):
    n_layers = lb_logits.shape[0]
    nt = seq // tblk
    w = hb * A_DK
    blk = lambda c0: pl.BlockSpec((tblk, w), lambda b, h, t: (b * nt + t, c0 // hb + h))
    return pl.pallas_call(
        functools.partial(_hgrn2_kernel, layer=layer, hb=hb, tblk=tblk),
        grid=(bsz, A_HEADS // hb, nt),
        in_specs=[
            pl.BlockSpec((n_layers, w), lambda b, h, t: (0, h)),
            pl.BlockSpec((1, w), lambda b, h, t: (0, h)),
            blk(col_q), blk(col_f), blk(col_v), blk(col_g),
        ],
        out_specs=pl.BlockSpec((tblk, w), lambda b, h, t: (b * nt + t, h)),
        out_shape=jax.ShapeDtypeStruct((bsz * seq, A_HEADS * A_DV), BF16),
        scratch_shapes=[pltpu.VMEM((hb, A_DV, A_DK), F32), pltpu.VMEM((hb, HG_CHUNK, A_DK), F32),
                        pltpu.VMEM((hb, HG_CHUNK, A_DK), F32), pltpu.VMEM((hb, HG_CHUNK, A_DV), F32)],
        compiler_params=_params("parallel", "parallel", "arbitrary"),
        name="hgrn2",
    )(lb_logits, gain.reshape(1, -1), p, p, p, p)


def _position_features(pos, lane):
    ab = jnp.where((lane & 1) == 0, lax.shift_right_logical(pos, 6), pos & 63)
    return jnp.where(lane >= LANE_ALIBI, jnp.where(lane < LANE_PAD, ab, 0), 0)


def _compress_block(a_ref, pe_ref, w1_ref, w2_ref, n_units, transposed=False):
    h_a = jnp.zeros((n_units, B_DH), F32)
    h_b = jnp.zeros((n_units, B_DH), F32)
    for r in range(CMP_STRIDE):
        a_r = a_ref[pl.ds(r, n_units, stride=CMP_STRIDE), :]
        h_a = h_a + jnp.dot((a_r + pe_ref[r:r + 1, :]).astype(BF16), w1_ref[r * B_DH:(r + 1) * B_DH, :],
                            preferred_element_type=F32)
        r2 = CMP_STRIDE + r
        h_b = h_b + jnp.dot((a_r + pe_ref[r2:r2 + 1, :]).astype(BF16), w1_ref[r2 * B_DH:(r2 + 1) * B_DH, :],
                            preferred_element_type=F32)
    hid = _silu(h_a + pltpu.roll(h_b, n_units - 1, 0)).astype(BF16)
    if transposed:
        return lax.dot_general(w2_ref[...], hid, _NT, preferred_element_type=F32)
    return jnp.dot(hid, w2_ref[...], preferred_element_type=F32)


def _nsa_kernel(feat_ref, q_ref, kcr_ref, vcr_ref, ks_ref, kw_ref, vst_ref, vwt_ref, bg_ref, gate_ref,
                pek_ref, w1k_ref, w2k_ref, pev_ref, w1v_ref, w2vt_ref,
                o_ref, ksa_ref, kwa_ref, vwp_ref, kca_ref, vct_ref, madd_ref, cpat_ref, score_ref, cin_ref, *,
                seq, gb):
    tq, kb_len, hpg, dh = NSA_TQ, NSA_KB, B_HPG, B_DH
    rows_all = hpg * tq
    n_slc = seq // SEL_LEN
    n_units = seq // CMP_STRIDE
    n_cmp = (seq - CMP_LEN) // CMP_STRIDE + 1
    qi = pl.program_id(2)
    t0 = pl.multiple_of(qi * tq, tq)

    @pl.when(qi == 0)
    def _():
        pos = lax.broadcasted_iota(jnp.int32, (seq, dh), 0)
        lane = lax.broadcasted_iota(jnp.int32, (seq, dh), 1)
        alibi = _position_features(pos, lane)
        blk_hot = jnp.where(lane == lax.shift_right_logical(pos, 6), 1, 0)
        chunk_hot = jnp.where(lane - LANE_CHUNK == lax.shift_right_logical(pos, 7), 1, 0)
        feat_s = (alibi + blk_hot + chunk_hot).astype(F32).astype(BF16)
        feat_w = alibi.astype(F32).astype(BF16)
        lane_w = lax.broadcasted_iota(jnp.int32, (WIN, dh), 1)
        pad_w = jnp.where(lane_w == LANE_PAD, 1.0, 0.0).astype(BF16)
        c_end = lax.broadcasted_iota(jnp.int32, (n_units, dh), 0) * CMP_STRIDE + (CMP_LEN - 1)
        lane_c = lax.broadcasted_iota(jnp.int32, (n_units, dh), 1)
        feat_c = _position_features(c_end, lane_c).astype(F32).astype(BF16)
        for gi in range(gb):
            gc = slice(gi * dh, (gi + 1) * dh)
            ksa_ref[gi, :, 0:dh] = ks_ref[:, gc].astype(BF16)
            ksa_ref[gi, :, dh:2 * dh] = feat_s
            kwa_ref[gi, WIN:WIN + seq, 0:dh] = kw_ref[:, gc].astype(BF16)
            kwa_ref[gi, WIN:WIN + seq, dh:2 * dh] = feat_w
            kwa_ref[gi, 0:WIN, 0:dh] = jnp.zeros((WIN, dh), BF16)
            kwa_ref[gi, 0:WIN, dh:2 * dh] = pad_w
            vwp_ref[gi, :, 0:WIN] = jnp.zeros((dh, WIN), BF16)
            vwp_ref[gi, :, WIN:WIN + seq] = vwt_ref[gc, :]
            cin_ref[...] = kcr_ref[:, gc]
            kca_ref[gi, :, 0:dh] = _compress_block(cin_ref, pek_ref, w1k_ref, w2k_ref, n_units).astype(BF16)
            kca_ref[gi, :, dh:2 * dh] = feat_c
            cin_ref[...] = vcr_ref[:, gc]
            vct_ref[gi] = _compress_block(cin_ref, pev_ref, w1v_ref, w2vt_ref, n_units,
                                          transposed=True).astype(BF16)
        j_loc = lax.broadcasted_iota(jnp.int32, (tq, rows_all), 0)
        t_loc = lax.broadcasted_iota(jnp.int32, (tq, rows_all), 1) & (tq - 1)
        madd_ref[0] = jnp.where(j_loc > t_loc, 0.0, -MASK_BIG)
        madd_ref[1] = jnp.where(j_loc <= t_loc, 0.0, -MASK_BIG)
        cpat_ref[...] = ((lax.broadcasted_iota(jnp.int32, (n_units, rows_all), 1) & (tq - 1))
                         - lax.broadcasted_iota(jnp.int32, (n_units, rows_all), 0) * CMP_STRIDE)

    def stack_heads(q2t, extras):
        return jnp.concatenate(
            [jnp.concatenate([q2t[h], extras[h].astype(BF16)], axis=0) for h in range(hpg)], axis=1)

    def softmax_cols(s):
        e = jnp.exp2(s - jnp.max(s, axis=0, keepdims=True))
        return e.astype(BF16), jnp.sum(e, axis=0, keepdims=True)

    row1 = lax.broadcasted_iota(jnp.int32, (dh, 1), 0)
    row_q = lax.broadcasted_iota(jnp.int32, (dh, tq), 0)
    span = WIN + tq
    gw = hpg * dh

    def before_loop(gi):
        qf = q_ref[:, gi * gw:(gi + 1) * gw]
        q2t = [jnp.transpose(qf[:, h * dh:(h + 1) * dh] * (dh ** -0.5 * LOG2E)).astype(BF16) for h in range(hpg)]
        feat = feat_ref[gi]
        q_w = stack_heads(q2t, [jnp.broadcast_to(feat[:, h:h + 1], (dh, tq)) for h in range(hpg)])

        s_w = jnp.dot(kwa_ref[gi, pl.ds(t0, span), :], q_w, preferred_element_type=F32)
        s_w = jnp.concatenate([s_w[0:tq] + madd_ref[0], s_w[tq:WIN], s_w[WIN:span] + madd_ref[1]], axis=0)
        e_w, l_w = softmax_cols(s_w)
        o_win = jnp.dot(vwp_ref[gi, :, pl.ds(t0, span)], e_w, preferred_element_type=F32) * (1.0 / l_w)

        mask_c = cpat_ref[...] >= (CMP_LEN - 1) - t0
        s_c = jnp.where(mask_c, jnp.dot(kca_ref[gi], q_w, preferred_element_type=F32), NEG_INF)
        e_c = jnp.where(mask_c, jnp.exp2(s_c - jnp.max(s_c, axis=0, keepdims=True)), 0.0)
        l_c = jnp.sum(e_c, axis=0, keepdims=True)
        p_c = e_c * (1.0 / jnp.where(l_c > 0.0, l_c, 1.0))
        o_cmp = jnp.dot(vct_ref[gi], p_c.astype(BF16), preferred_element_type=F32)

        p_sum = p_c[:, 0:tq]
        for h in range(1, hpg):
            p_sum = p_sum + p_c[:, h * tq:(h + 1) * tq]
        jn = lax.broadcasted_iota(jnp.int32, (n_slc, n_units), 0) * SEL_LEN
        cn = lax.broadcasted_iota(jnp.int32, (n_slc, n_units), 1) * CMP_STRIDE
        ov_t = jnp.where(cn <= jn + (SEL_LEN - 1),
                         jnp.where(cn + (CMP_LEN - 1) >= jn, jnp.where(cn < n_cmp * CMP_STRIDE, 1.0, 0.0), 0.0),
                         0.0).astype(BF16)
        p_hi = p_sum.astype(BF16)
        p_lo = (p_sum - p_hi.astype(F32)).astype(BF16)
        imp_t = (jnp.dot(ov_t, p_hi, preferred_element_type=F32)
                 + jnp.dot(ov_t, p_lo, preferred_element_type=F32))
        blk = lax.broadcasted_iota(jnp.int32, (n_slc, tq), 0)
        cur = lax.shift_right_logical(t0 + lax.broadcasted_iota(jnp.int32, (n_slc, tq), 1), 6)
        forced = jnp.where(blk == 0, 1, jnp.where(blk == cur, 1, jnp.where(blk == cur - 1, 1, 0)))
        score = jnp.where(forced > 0, FORCE_SCORE, jnp.where(blk <= cur, imp_t, NEG_INF))
        score_ref[gi] = score
        rank = jnp.zeros((n_slc, tq), jnp.int32)
        for j in range(n_slc):
            sj = score_ref[gi, pl.ds(j, 1), :]
            rank = rank + jnp.where(sj > score, 1, jnp.where(blk > j, jnp.where(sj == score, 1, 0), 0))
        sel_bias_t = jnp.where(rank < SEL_TOPK, 0.0, -MASK_BIG)
        sel_bias = jnp.concatenate([sel_bias_t, jnp.zeros((dh - n_slc, tq), F32)], axis=0)

        feat_np = jnp.where(row1 == LANE_PAD, 0.0, feat)
        chunk_col = jnp.where(row1 >= LANE_CHUNK,
                              jnp.where(row1 < LANE_CHUNK + seq // tq,
                                        jnp.where(row1 - LANE_CHUNK < qi, 0.0, -MASK_BIG), 0.0), 0.0)
        extra_d = [jnp.where(row_q < n_slc, sel_bias, feat_np[:, h:h + 1]) for h in range(hpg)]
        q_d = stack_heads(q2t, extra_d)
        q_s = stack_heads(q2t, [x + chunk_col for x in extra_d])
        s_d = jnp.dot(ksa_ref[gi, pl.ds(t0, tq), :], q_d, preferred_element_type=F32) + madd_ref[1]
        m0 = jnp.max(s_d, axis=0, keepdims=True)
        e_d = jnp.exp2(s_d - m0)
        l0 = jnp.sum(e_d, axis=0, keepdims=True)
        acc0 = jnp.dot(vst_ref[gi * dh:(gi + 1) * dh, pl.ds(t0, tq)], e_d.astype(BF16),
                       preferred_element_type=F32)
        return o_win, o_cmp, q_s, (m0, l0, acc0)

    pre = [before_loop(gi) for gi in range(gb)]

    def sel_step(kb, carry):
        k0 = pl.multiple_of(kb * kb_len, kb_len)
        out = []
        for gi in range(gb):
            m, l, acc = carry[gi]
            s = jnp.dot(ksa_ref[gi, pl.ds(k0, kb_len), :], pre[gi][2], preferred_element_type=F32)
            m_new = jnp.maximum(m, jnp.max(s, axis=0, keepdims=True))
            alpha = jnp.exp2(m - m_new)
            e = jnp.exp2(s - m_new)
            l = alpha * l + jnp.sum(e, axis=0, keepdims=True)
            acc = alpha * acc + jnp.dot(vst_ref[gi * dh:(gi + 1) * dh, pl.ds(k0, kb_len)], e.astype(BF16),
                                        preferred_element_type=F32)
            out.append((m_new, l, acc))
        return tuple(out)

    n_kb = lax.shift_right_logical(t0 + kb_len - 1, int(math.log2(kb_len)))
    sel = lax.fori_loop(0, n_kb, sel_step, tuple(p[3] for p in pre))

    for gi in range(gb):
        o_win, o_cmp = pre[gi][0], pre[gi][1]
        _, l_s, acc_s = sel[gi]
        o_sel = acc_s * (1.0 / l_s)
        sg_t = jnp.transpose(jax.nn.sigmoid(gate_ref[:, gi * dh:(gi + 1) * dh]))
        grow = lambda br: jnp.concatenate([sg_t[3 * h + br:3 * h + br + 1, :] for h in range(hpg)], axis=1)
        o = grow(0) * o_cmp + grow(1) * o_sel + grow(2) * o_win
        for h in range(hpg):
            c0 = gi * gw + h * dh
            o_h = jnp.transpose(o[:, h * tq:(h + 1) * tq])
            o_ref[:, c0:c0 + dh] = (o_h * _silu(bg_ref[:, c0:c0 + dh])).astype(o_ref.dtype)


def _slope_features():
    slopes = jnp.asarray(2.0 ** (-8.0 * np.arange(1, B_HEADS + 1) / B_HEADS) * LOG2E, dtype=F32)
    hi = slopes.astype(BF16).astype(F32)
    mid = (slopes - hi).astype(BF16).astype(F32)
    lo = (slopes - hi - mid).astype(BF16).astype(F32)
    cols = jnp.stack([64.0 * hi, hi, 64.0 * mid, mid, 64.0 * lo, lo, jnp.full_like(hi, -MASK_BIG)], axis=1)
    feat = jnp.zeros((B_HEADS, B_DH), F32).at[:, LANE_ALIBI:LANE_PAD + 1].set(cols)
    feat = feat.reshape(B_KV, B_HPG, B_DH)
    return jnp.swapaxes(jnp.pad(feat, ((0, 0), (0, 8 - B_HPG), (0, 0))), 1, 2)


def nsa(p, pt, pvt, pe_k, w1_k, w2_k, pe_v, w1_v, w2t_v, layer, bsz, seq,
        col_q, col_kc, col_vc, col_ks, col_kw, col_bg, col_gate):
    nq = seq // NSA_TQ
    gb = NSA_GB
    gw = B_HPG * B_DH
    n_units = seq // CMP_STRIDE
    rows_all = B_HPG * NSA_TQ
    assert all(c % gb == 0 for c in (col_q, col_kc, col_vc, col_ks, col_kw, col_bg, col_gate)) and B_KV % gb == 0
    tile = lambda c0, w: pl.BlockSpec((NSA_TQ, gb * w), lambda b, g, i: (b * nq + i, c0 // gb + g))
    full = lambda c0: pl.BlockSpec((seq, gb * B_DH), lambda b, g, i: (b, c0 // gb + g))
    full_t = lambda r0: pl.BlockSpec((gb * B_DH, seq), lambda b, g, i: (r0 // gb + g, b))
    wspec = lambda a: pl.BlockSpec((None,) + a.shape[1:], lambda b, g, i: (layer,) + (0,) * (a.ndim - 1))
    return pl.pallas_call(
        functools.partial(_nsa_kernel, seq=seq, gb=gb),
        grid=(bsz, B_KV // gb, nq),
        in_specs=[
            pl.BlockSpec((gb, B_DH, 8), lambda b, g, i: (g, 0, 0)),
            tile(col_q, gw), full(col_kc), full(col_vc), full(col_ks), full(col_kw),
            full_t(0), full_t(B_KV),
            tile(col_bg, gw), tile(col_gate, B_DH),
            wspec(pe_k), wspec(w1_k), wspec(w2_k), wspec(pe_v), wspec(w1_v), wspec(w2t_v),
        ],
        out_specs=pl.BlockSpec((NSA_TQ, gb * gw), lambda b, g, i: (b * nq + i, g)),
        out_shape=jax.ShapeDtypeStruct((bsz * seq, B_HEADS * B_DH), BF16),
        scratch_shapes=[
            pltpu.VMEM((gb, seq, 2 * B_DH), BF16),
            pltpu.VMEM((gb, seq + WIN, 2 * B_DH), BF16),
            pltpu.VMEM((gb, B_DH, seq + WIN), BF16),
            pltpu.VMEM((gb, n_units, 2 * B_DH), BF16),
            pltpu.VMEM((gb, B_DH, n_units), BF16),
            pltpu.VMEM((2, NSA_TQ, rows_all), F32),
            pltpu.VMEM((n_units, rows_all), jnp.int32),
            pltpu.VMEM((gb, seq // SEL_LEN, NSA_TQ), F32),
            pltpu.VMEM((seq, B_DH), F32),
        ],
        compiler_params=_params("parallel", "parallel", "arbitrary"),
        name="nsa_attention",
    )(_slope_features(), p, p, p, p, p, pvt, pvt, pt, pt, pe_k, w1_k, w2_k, pe_v, w1_v, w2t_v)


def _softplus(x):
    return jnp.maximum(x, 0.0) + jnp.log1p(jnp.exp(-jnp.abs(x)))


def _rglru_kernel(xb_ref, g_ref, cw_ref, cb_ref, wa_ref, ba_ref, wi_ref, bi_ref, lam_ref, o_ref,
                  xpad_ref, h_ref, ga_ref, gu_ref, gc_ref):
    tb = RG_TB
    pad = 8

    @pl.when(pl.program_id(2) == 0)
    def _():
        xpad_ref[0:pad, :] = jnp.zeros((pad, RG_BS), F32)
        h_ref[...] = jnp.zeros_like(h_ref)

    x = xb_ref[...]
    xpad_ref[pad:pad + tb, :] = x
    xc = xpad_ref[pl.ds(pad - 3, tb), :] * cw_ref[0:1, :]
    for j in range(1, CONV_W):
        xc = xc + xpad_ref[pl.ds(pad - 3 + j, tb), :] * cw_ref[j:j + 1, :]
    xc = xc + cb_ref[...]
    xpad_ref[0:pad, :] = x[tb - pad:tb]

    xc_b = xc.astype(BF16)
    r = jax.nn.sigmoid(jnp.dot(xc_b, wa_ref[...], preferred_element_type=F32) + ba_ref[...])
    i = jax.nn.sigmoid(jnp.dot(xc_b, wi_ref[...], preferred_element_type=F32) + bi_ref[...])
    log_a = -RG_C * _softplus(-lam_ref[...]) * r
    a = jnp.exp(log_a)
    th = jnp.tanh(log_a)
    u = jnp.sqrt(jnp.maximum(-2.0 * th / (1.0 - th), 0.0)) * (i * xc)

    def scan_rows(a, u, row, n, axis):
        sh = 1
        while sh < n:
            keep = row >= sh
            a_s = jnp.where(keep, pltpu.roll(a, sh, axis), 1.0)
            u_s = jnp.where(keep, pltpu.roll(u, sh, axis), 0.0)
            u = a * u_s + u
            a = a * a_s
            sh *= 2
        return a, u

    grp = 8
    ng = tb // grp
    a, u = scan_rows(a.reshape(ng, grp, RG_BS), u.reshape(ng, grp, RG_BS),
                     lax.broadcasted_iota(jnp.int32, (ng, grp, RG_BS), 1), grp, 1)
    a, u = a.reshape(tb, RG_BS), u.reshape(tb, RG_BS)
    row_g = lax.broadcasted_iota(jnp.int32, (ng, 128), 0)
    h_prev = h_ref[...]
    pieces = []
    for k in range(RG_BS // 128):
        lanes = slice(k * 128, (k + 1) * 128)
        ga_ref[k] = a[:, lanes]
        gu_ref[k] = u[:, lanes]
        a_g, u_g = scan_rows(ga_ref[k, pl.ds(grp - 1, ng, stride=grp), :],
                             gu_ref[k, pl.ds(grp - 1, ng, stride=grp), :], row_g, ng, 0)
        h_end = u_g + a_g * h_prev[:, lanes]
        h_ref[:, lanes] = h_end[ng - 1:ng]
        gc_ref[k] = jnp.where(row_g >= 1, pltpu.roll(h_end, 1, 0), h_prev[:, lanes])
        pieces.append(jnp.concatenate(
            [u[g * grp:(g + 1) * grp, lanes] + a[g * grp:(g + 1) * grp, lanes] * gc_ref[k, pl.ds(g, 1), :]
             for g in range(ng)], axis=0))
    h = jnp.concatenate(pieces, axis=1)
    o_ref[...] = (h * _silu(g_ref[...])).astype(o_ref.dtype)


def rglru(p, conv_w, conv_b, w_a, b_a, w_i, b_i, lam, layer, bsz, seq):
    nt = seq // RG_TB
    d_rnn = RG_BLOCKS * RG_BS
    tile = lambda c0: pl.BlockSpec((RG_TB, RG_BS), lambda b, n, t: (b * nt + t, c0 + n))
    vec = lambda rows: pl.BlockSpec((None, rows, RG_BS), lambda b, n, t: (layer, 0, n))
    mat = pl.BlockSpec((None, None, RG_BS, RG_BS), lambda b, n, t: (layer, n, 0, 0))
    n_layers = conv_b.shape[0]
    row = lambda a: a.reshape(n_layers, 1, d_rnn)
    return pl.pallas_call(
        _rglru_kernel,
        grid=(bsz, RG_BLOCKS, nt),
        in_specs=[tile(0), tile(RG_BLOCKS), vec(CONV_W), vec(1), mat, vec(1), mat, vec(1), vec(1)],
        out_specs=pl.BlockSpec((RG_TB, RG_BS), lambda b, n, t: (b * nt + t, n)),
        out_shape=jax.ShapeDtypeStruct((bsz * seq, d_rnn), BF16),
        scratch_shapes=[pltpu.VMEM((RG_TB + 8, RG_BS), F32), pltpu.VMEM((1, RG_BS), F32),
                        pltpu.VMEM((RG_BS // 128, RG_TB, 128), F32), pltpu.VMEM((RG_BS // 128, RG_TB, 128), F32),
                        pltpu.VMEM((RG_BS // 128, RG_TB // 8, 128), F32)],
        compiler_params=_params("parallel", "parallel", "arbitrary"),
        name="rglru",
    )(p, p, conv_w, row(conv_b), w_a, row(b_a), w_i, row(b_i), row(lam))


def _even_tail_weight(w, sizes):
    start = int(np.sum(sizes[:-2]))
    n_gate, d_b = sizes[-2], sizes[-1]
    gate = w[:, :, start:start + n_gate].reshape(w.shape[0], w.shape[1], B_KV, 3 * B_HPG)
    gate = jnp.pad(gate, ((0, 0), (0, 0), (0, 0), (0, B_DH - 3 * B_HPG))).reshape(w.shape[0], w.shape[1], B_KV * B_DH)
    return jnp.concatenate([w[:, :, start + n_gate:start + n_gate + d_b], gate], axis=2).astype(BF16)


def kernel(x, norm_w, final_norm_w, even_w_in, even_w_out, hgrn_lb_logits, hgrn_norm_w, cmp_pe_k, cmp_w1_k, cmp_w2_k, cmp_pe_v, cmp_w1_v, cmp_w2_v, odd_w_in, odd_w_out, rg_conv_w, rg_conv_b, rg_w_a, rg_b_a, rg_w_i, rg_b_i, rg_lambda):
    bsz, seq, d_model = x.shape
    depth = norm_w.shape[0]
    d_a = A_HEADS * A_DV
    d_b = B_HEADS * B_DH
    d_rnn = RG_BLOCKS * RG_BS
    a_qk = A_HEADS * A_DK
    kvw = B_KV * B_DH
    gw = B_HPG * B_DH
    assert seq % NSA_KB == 0 and seq % RG_TB == 0 and seq % HG_CHUNK == 0 and CMP_LEN == 2 * CMP_STRIDE
    assert seq // SEL_LEN <= LANE_ALIBI and seq // NSA_TQ <= B_DH - LANE_CHUNK and seq >= WIN + NSA_TQ
    sizes = (a_qk, a_qk, d_a, d_a, d_b, kvw, kvw, kvw, kvw, kvw, kvw, 3 * B_HEADS, d_b)
    names = ("a_q", "a_f", "a_i", "a_g", "b_q", "b_kc", "b_vc", "b_ks", "b_vs", "b_kw", "b_vw", "b_gate", "b_g")
    off = dict(zip(names, np.concatenate([[0], np.cumsum(sizes)[:-1]]).tolist()))
    tn = 512
    assert all(off[k] % tn == 0 for k in ("b_vs", "b_kw", "b_vw")) and kvw == tn
    skip_tile = off["b_vs"] // tn
    n_main_tiles = off["b_vw"] // tn - 1
    col_kw = (off["b_kw"] - tn) // B_DH

    even_in_b = even_w_in[:, :, :off["b_vw"]].astype(BF16)
    even_tail_b = _even_tail_weight(even_w_in, sizes)
    vcols = lambda name: jnp.swapaxes(even_w_in[:, :, off[name]:off[name] + kvw], 1, 2)
    even_vt_b = jnp.concatenate([vcols("b_vs"), vcols("b_vw")], axis=1).astype(BF16)
    even_out_b = even_w_out.astype(BF16)
    odd_in_b = odd_w_in.astype(BF16)
    odd_out_b = odd_w_out.astype(BF16)
    w_a_b, w_i_b = rg_w_a.astype(BF16), rg_w_i.astype(BF16)
    w1k_b, w2k_b = cmp_w1_k.astype(BF16), cmp_w2_k.astype(BF16)
    w1v_b, w2tv_b = cmp_w1_v.astype(BF16), jnp.swapaxes(cmp_w2_v, 1, 2).astype(BF16)

    xf = x.reshape(bsz * seq, d_model)
    for layer in range(depth):
        if layer % 2 == 0:
            e = layer // 2
            p = norm_matmul(xf, norm_w[layer], even_in_b, e, n_main_tiles, skip_tile)
            pt = norm_matmul(xf, norm_w[layer], even_tail_b, e, even_tail_b.shape[2] // tn)
            pvt = norm_matmul_t(xf, norm_w[layer], even_vt_b, e)
            ya = hgrn2(p, hgrn_lb_logits, hgrn_norm_w[e], e, bsz, seq,
                       off["a_q"] // A_DK, off["a_f"] // A_DK, off["a_i"] // A_DV, off["a_g"] // A_DV)
            yb = nsa(p, pt, pvt, cmp_pe_k, w1k_b, w2k_b, cmp_pe_v, w1v_b, w2tv_b, e, bsz, seq,
                     off["b_q"] // gw, off["b_kc"] // B_DH, off["b_vc"] // B_DH, off["b_ks"] // B_DH,
                     col_kw, 0, d_b // B_DH)
            xf = proj_residual([ya, yb], even_out_b, e, xf)
        else:
            o = layer // 2
            p = norm_matmul(xf, norm_w[layer], odd_in_b, o, 2 * d_rnn // tn)
            hr = rglru(p, rg_conv_w, rg_conv_b, w_a_b, rg_b_a, w_i_b, rg_b_i, rg_lambda, o, bsz, seq)
            xf = proj_residual([hr], odd_out_b, o, xf)
    return rmsnorm(xf, final_norm_w).reshape(bsz, seq, d_model)
```

```python
import functools
import math

import jax
import jax.numpy as jnp
import numpy as np
from jax import lax
from jax.experimental import pallas as pl
from jax.experimental.pallas import tpu as pltpu

F32 = jnp.float32
BF16 = jnp.bfloat16

NORM_EPS = 1e-6
NEG_INF = -1e30
FORCE_SCORE = 1e30
TINY = 1e-30
LOG2E = 1.4426950408889634

A_HEADS = 16
A_DK = 128
A_DV = 128
HG_CHUNK = 64
HG_SUB = 8

B_HEADS = 16
B_DH = 128
B_KV = 4
B_HPG = B_HEADS // B_KV
CMP_LEN = 32
CMP_STRIDE = 16
SEL_LEN = 64
SEL_TOPK = 16
WIN = 512
NSA_TQ = 128
NSA_KB = 512
NSA_GB = 2
MASK_BIG = 32768.0

LANE_ALIBI = 32
LANE_PAD = 38
LANE_CHUNK = 40

RG_BLOCKS = 10
RG_BS = 256
CONV_W = 4
RG_C = 8.0
RG_TB = 256

VMEM_LIMIT = 56 * 1024 * 1024

_NT = (((1,), (1,)), ((), ()))
_TN = (((0,), (0,)), ((), ()))


def _silu(x):
    return x * jax.nn.sigmoid(x)


def _params(*sem):
    return pltpu.CompilerParams(dimension_semantics=sem, vmem_limit_bytes=VMEM_LIMIT)


def _norm_matmul_kernel(x_ref, nw_ref, w_ref, o_ref, h_ref):
    @pl.when(pl.program_id(1) == 0)
    def _():
        x = x_ref[...]
        ms = jnp.mean(x * x, axis=-1, keepdims=True)
        h_ref[...] = (x * lax.rsqrt(ms + NORM_EPS) * nw_ref[...]).astype(BF16)

    o_ref[...] = jnp.dot(h_ref[...], w_ref[...].astype(BF16), preferred_element_type=F32)


def norm_matmul(x, nw, w, layer, n_tiles, skip_tile=None, tm=1024, tn=512):
    m, d = x.shape
    src = (lambda j: j) if skip_tile is None else (lambda j: j + j // skip_tile)
    assert skip_tile is None or n_tiles <= 2 * skip_tile
    return pl.pallas_call(
        _norm_matmul_kernel,
        grid=(m // tm, n_tiles),
        in_specs=[
            pl.BlockSpec((tm, d), lambda i, j: (i, 0)),
            pl.BlockSpec((1, d), lambda i, j: (0, 0)),
            pl.BlockSpec((None, d, tn), lambda i, j: (layer, 0, src(j))),
        ],
        out_specs=pl.BlockSpec((tm, tn), lambda i, j: (i, j)),
        out_shape=jax.ShapeDtypeStruct((m, n_tiles * tn), F32),
        scratch_shapes=[pltpu.VMEM((tm, d), BF16)],
        compiler_params=_params("parallel", "arbitrary"),
        name="norm_matmul",
    )(x, nw.reshape(1, d), w)


def _norm_matmul_t_kernel(x_ref, nw_ref, wt_ref, o_ref, h_ref):
    @pl.when(pl.program_id(1) == 0)
    def _():
        x = x_ref[...]
        ms = jnp.mean(x * x, axis=-1, keepdims=True)
        h_ref[...] = (x * lax.rsqrt(ms + NORM_EPS) * nw_ref[...]).astype(BF16)

    o_ref[...] = lax.dot_general(wt_ref[...], h_ref[...], _NT, preferred_element_type=F32).astype(o_ref.dtype)


def norm_matmul_t(x, nw, wt, layer, tm=1024, tn=512):
    m, d = x.shape
    n = wt.shape[1]
    return pl.pallas_call(
        _norm_matmul_t_kernel,
        grid=(m // tm, n // tn),
        in_specs=[
            pl.BlockSpec((tm, d), lambda i, j: (i, 0)),
            pl.BlockSpec((1, d), lambda i, j: (0, 0)),
            pl.BlockSpec((None, tn, d), lambda i, j: (layer, j, 0)),
        ],
        out_specs=pl.BlockSpec((tn, tm), lambda i, j: (j, i)),
        out_shape=jax.ShapeDtypeStruct((n, m), BF16),
        scratch_shapes=[pltpu.VMEM((tm, d), BF16)],
        compiler_params=_params("parallel", "arbitrary"),
        name="norm_matmul_t",
    )(x, nw.reshape(1, d), wt)


def _proj_residual_kernel(*refs, n_in):
    a_refs, w_refs = refs[:n_in], refs[n_in:2 * n_in]
    x_ref, o_ref = refs[2 * n_in], refs[2 * n_in + 1]
    y = jnp.dot(a_refs[0][...], w_refs[0][...].astype(BF16), preferred_element_type=F32)
    for a_ref, w_ref in zip(a_refs[1:], w_refs[1:]):
        y = y + jnp.dot(a_ref[...], w_ref[...].astype(BF16), preferred_element_type=F32)
    o_ref[...] = x_ref[...] + y


def proj_residual(acts, w, layer, x, tm=1024, tn=512):
    m, n = x.shape
    n_in = len(acts)
    k = acts[0].shape[1]
    in_specs = [pl.BlockSpec((tm, k), lambda i, j: (i, 0)) for _ in acts]
    in_specs += [pl.BlockSpec((None, k, tn), functools.partial(lambda i, j, s: (layer, s, j), s=s))
                 for s in range(n_in)]
    in_specs += [pl.BlockSpec((tm, tn), lambda i, j: (i, j))]
    return pl.pallas_call(
        functools.partial(_proj_residual_kernel, n_in=n_in),
        grid=(m // tm, n // tn),
        in_specs=in_specs,
        out_specs=pl.BlockSpec((tm, tn), lambda i, j: (i, j)),
        out_shape=jax.ShapeDtypeStruct((m, n), F32),
        compiler_params=_params("parallel", "arbitrary"),
        name="proj_residual",
    )(*acts, *([w] * n_in), x)


def _rmsnorm_kernel(x_ref, w_ref, o_ref):
    x = x_ref[...]
    ms = jnp.mean(x * x, axis=-1, keepdims=True)
    o_ref[...] = x * lax.rsqrt(ms + NORM_EPS) * w_ref[...]


def rmsnorm(x, w, tm=512):
    m, d = x.shape
    return pl.pallas_call(
        _rmsnorm_kernel,
        grid=(m // tm,),
        in_specs=[pl.BlockSpec((tm, d), lambda i: (i, 0)), pl.BlockSpec((1, d), lambda i: (0, 0))],
        out_specs=pl.BlockSpec((tm, d), lambda i: (i, 0)),
        out_shape=jax.ShapeDtypeStruct((m, d), F32),
        compiler_params=_params("parallel"),
        name="final_rmsnorm",
    )(x, w.reshape(1, d))


def _split3(x):
    hi = x.astype(BF16)
    r1 = x - hi.astype(F32)
    mid = r1.astype(BF16)
    lo = (r1 - mid.astype(F32)).astype(BF16)
    return hi, mid, lo


def _hgrn2_levels(c_len, sub):
    levels, seg, half = [], 0, sub
    while half < c_len:
        levels.append((half, seg))
        seg += c_len // (2 * half)
        half *= 2
    return levels, seg


def _hgrn2_kernel(lbl_ref, gain_ref, q_ref, f_ref, v_ref, g_ref, o_ref, st_ref, b_ref, k_ref, vc_ref, *,
                  layer, hb, tblk):
    c_len, sub = HG_CHUNK, HG_SUB
    n_sub = c_len // sub
    levels, n_seg = _hgrn2_levels(c_len, sub)
    lg = lbl_ref[...]
    ex = jnp.exp(lg - jnp.max(lg, axis=0, keepdims=True))
    sm = ex / jnp.sum(ex, axis=0, keepdims=True)
    lb_all = sm[0:1]
    for j in range(1, layer + 1):
        lb_all = lb_all + sm[j:j + 1]
    lb_all = lb_all - sm[0:1]
    gain_all = gain_ref[...]

    @pl.when(pl.program_id(2) == 0)
    def _():
        st_ref[...] = jnp.zeros_like(st_ref)

    tri = jnp.where(lax.broadcasted_iota(jnp.int32, (c_len, c_len), 0)
                    >= lax.broadcasted_iota(jnp.int32, (c_len, c_len), 1), 1.0, 0.0).astype(BF16)
    ones_kk = jnp.ones((A_DK, A_DK), BF16)
    row_sub = lax.broadcasted_iota(jnp.int32, (sub, A_DK), 0)
    zero_blk = jnp.zeros((sub, A_DK), F32)

    def head_chunk(h, r0):
        rows = pl.ds(r0, c_len)
        cols = slice(h * A_DK, (h + 1) * A_DK)
        lb = lb_all[:, cols]
        one_m_lb = 1.0 - lb
        q = _silu(q_ref[rows, cols])
        fz = f_ref[rows, cols]
        v = v_ref[rows, cols]
        vc_ref[h] = v
        v_b = v.astype(BF16)
        f = lb + one_m_lb * jax.nn.sigmoid(fz)
        hi, mid, lo = _split3(jnp.log2(jnp.maximum(f, TINY)))
        kk = one_m_lb * jax.nn.sigmoid(-fz)
        b = (jnp.dot(tri, hi, preferred_element_type=F32)
             + jnp.dot(tri, mid, preferred_element_type=F32)
             + jnp.dot(tri, lo, preferred_element_type=F32))
        b_ref[h] = b
        k_ref[h] = kk
        st = st_ref[h]
        o_inter = lax.dot_general((q * jnp.exp2(b)).astype(BF16), st.astype(BF16), _NT,
                                  preferred_element_type=F32)

        es = []
        for j in range(n_sub):
            bj = b[j * sub:(j + 1) * sub]
            qj = q[j * sub:(j + 1) * sub]
            for s in range(sub):
                bs = b_ref[h, pl.ds(j * sub + s, 1), :]
                ks = k_ref[h, pl.ds(j * sub + s, 1), :]
                es.append(qj * jnp.exp2(jnp.where(row_sub >= s, bj - bs, NEG_INF)) * ks)
        att_d = jnp.dot(jnp.concatenate(es, axis=0).astype(BF16), ones_kk, preferred_element_type=F32)
        o_diag = []
        for j in range(n_sub):
            acc = None
            for s in range(sub):
                u = j * sub + s
                term = att_d[u * sub:(u + 1) * sub] * vc_ref[h, pl.ds(u, 1), :]
                acc = term if acc is None else acc + term
            o_diag.append(acc)

        q_rows, k_rows = [], []
        for j in range(n_sub):
            bj = b[j * sub:(j + 1) * sub]
            q_seg = [zero_blk] * n_seg
            k_seg = [zero_blk] * n_seg
            for half, seg0 in levels:
                blk = (j * sub) // (2 * half)
                m_row = b_ref[h, pl.ds(blk * 2 * half + half - 1, 1), :]
                if ((j * sub) // half) % 2 == 1:
                    q_seg[seg0 + blk] = q[j * sub:(j + 1) * sub] * jnp.exp2(bj - m_row)
                else:
                    k_seg[seg0 + blk] = kk[j * sub:(j + 1) * sub] * jnp.exp2(m_row - bj)
            q_rows.append(jnp.concatenate(q_seg, axis=1))
            k_rows.append(jnp.concatenate(k_seg, axis=1))
        att_o = lax.dot_general(jnp.concatenate(q_rows, axis=0).astype(BF16),
                                jnp.concatenate(k_rows, axis=0).astype(BF16), _NT,
                                preferred_element_type=F32)
        o = (o_inter + jnp.dot(att_o.astype(BF16), v_b, preferred_element_type=F32)
             + jnp.concatenate(o_diag, axis=0))

        b_last = b_ref[h, pl.ds(c_len - 1, 1), :]
        k_dec = (kk * jnp.exp2(b_last - b)).astype(BF16)
        st_ref[h] = st * jnp.exp2(b_last) + lax.dot_general(v_b, k_dec, _TN, preferred_element_type=F32)

        o = o * lax.rsqrt(jnp.mean(o * o, axis=-1, keepdims=True) + NORM_EPS) * gain_all[:, cols]
        o_ref[rows, cols] = (o * _silu(g_ref[rows, cols])).astype(o_ref.dtype)

    def chunk(c, carry):
        r0 = pl.multiple_of(c * c_len, c_len)
        for h in range(hb):
            head_chunk(h, r0)
        return carry

    lax.fori_loop(0, tblk // c_len, chunk, 0)


def hgrn2(p, lb_logits, gain, layer, bsz, seq, col_q, col_f, col_v, col_g, hb=16, tblk=128):
    n_layers = lb_logits.shape[0]
    nt = seq // tblk
    w = hb * A_DK
    blk = lambda c0: pl.BlockSpec((tblk, w), lambda b, h, t: (b * nt + t, c0 // hb + h))
    return pl.pallas_call(
        functools.partial(_hgrn2_kernel, layer=layer, hb=hb, tblk=tblk),
        grid=(bsz, A_HEADS // hb, nt),
        in_specs=[
            pl.BlockSpec((n_layers, w), lambda b, h, t: (0, h)),
            pl.BlockSpec((1, w), lambda b, h, t: (0, h)),
            blk(col_q), blk(col_f), blk(col_v), blk(col_g),
        ],
        out_specs=pl.BlockSpec((tblk, w), lambda b, h, t: (b * nt + t, h)),
        out_shape=jax.ShapeDtypeStruct((bsz * seq, A_HEADS * A_DV), BF16),
        scratch_shapes=[pltpu.VMEM((hb, A_DV, A_DK), F32), pltpu.VMEM((hb, HG_CHUNK, A_DK), F32),
                        pltpu.VMEM((hb, HG_CHUNK, A_DK), F32), pltpu.VMEM((hb, HG_CHUNK, A_DV), F32)],
        compiler_params=_params("parallel", "parallel", "arbitrary"),
        name="hgrn2",
    )(lb_logits, gain.reshape(1, -1), p, p, p, p)


def _position_features(pos, lane):
    ab = jnp.where((lane & 1) == 0, lax.shift_right_logical(pos, 6), pos & 63)
    return jnp.where(lane >= LANE_ALIBI, jnp.where(lane < LANE_PAD, ab, 0), 0)


def _compress_block(a_ref, pe_ref, w1_ref, w2_ref, n_units, transposed=False):
    h_a = jnp.zeros((n_units, B_DH), F32)
    h_b = jnp.zeros((n_units, B_DH), F32)
    for r in range(CMP_STRIDE):
        a_r = a_ref[pl.ds(r, n_units, stride=CMP_STRIDE), :]
        h_a = h_a + jnp.dot((a_r + pe_ref[r:r + 1, :]).astype(BF16), w1_ref[r * B_DH:(r + 1) * B_DH, :],
                            preferred_element_type=F32)
        r2 = CMP_STRIDE + r
        h_b = h_b + jnp.dot((a_r + pe_ref[r2:r2 + 1, :]).astype(BF16), w1_ref[r2 * B_DH:(r2 + 1) * B_DH, :],
                            preferred_element_type=F32)
    hid = _silu(h_a + pltpu.roll(h_b, n_units - 1, 0)).astype(BF16)
    if transposed:
        return lax.dot_general(w2_ref[...], hid, _NT, preferred_element_type=F32)
    return jnp.dot(hid, w2_ref[...], preferred_element_type=F32)


def _nsa_kernel(feat_ref, q_ref, kcr_ref, vcr_ref, ks_ref, kw_ref, vst_ref, vwt_ref, bg_ref, gate_ref,
                pek_ref, w1k_ref, w2k_ref, pev_ref, w1v_ref, w2vt_ref,
                o_ref, ksa_ref, kwa_ref, vwp_ref, kca_ref, vct_ref, madd_ref, cpat_ref, score_ref, cin_ref, *,
                seq, gb):
    tq, kb_len, hpg, dh = NSA_TQ, NSA_KB, B_HPG, B_DH
    rows_all = hpg * tq
    n_slc = seq // SEL_LEN
    n_units = seq // CMP_STRIDE
    n_cmp = (seq - CMP_LEN) // CMP_STRIDE + 1
    qi = pl.program_id(2)
    t0 = pl.multiple_of(qi * tq, tq)

    @pl.when(qi == 0)
    def _():
        pos = lax.broadcasted_iota(jnp.int32, (seq, dh), 0)
        lane = lax.broadcasted_iota(jnp.int32, (seq, dh), 1)
        alibi = _position_features(pos, lane)
        blk_hot = jnp.where(lane == lax.shift_right_logical(pos, 6), 1, 0)
        chunk_hot = jnp.where(lane - LANE_CHUNK == lax.shift_right_logical(pos, 7), 1, 0)
        feat_s = (alibi + blk_hot + chunk_hot).astype(F32).astype(BF16)
        feat_w = alibi.astype(F32).astype(BF16)
        lane_w = lax.broadcasted_iota(jnp.int32, (WIN, dh), 1)
        pad_w = jnp.where(lane_w == LANE_PAD, 1.0, 0.0).astype(BF16)
        c_end = lax.broadcasted_iota(jnp.int32, (n_units, dh), 0) * CMP_STRIDE + (CMP_LEN - 1)
        lane_c = lax.broadcasted_iota(jnp.int32, (n_units, dh), 1)
        feat_c = _position_features(c_end, lane_c).astype(F32).astype(BF16)
        for gi in range(gb):
            gc = slice(gi * dh, (gi + 1) * dh)
            ksa_ref[gi, :, 0:dh] = ks_ref[:, gc].astype(BF16)
            ksa_ref[gi, :, dh:2 * dh] = feat_s
            kwa_ref[gi, WIN:WIN + seq, 0:dh] = kw_ref[:, gc].astype(BF16)
            kwa_ref[gi, WIN:WIN + seq, dh:2 * dh] = feat_w
            kwa_ref[gi, 0:WIN, 0:dh] = jnp.zeros((WIN, dh), BF16)
            kwa_ref[gi, 0:WIN, dh:2 * dh] = pad_w
            vwp_ref[gi, :, 0:WIN] = jnp.zeros((dh, WIN), BF16)
            vwp_ref[gi, :, WIN:WIN + seq] = vwt_ref[gc, :]
            cin_ref[...] = kcr_ref[:, gc]
            kca_ref[gi, :, 0:dh] = _compress_block(cin_ref, pek_ref, w1k_ref, w2k_ref, n_units).astype(BF16)
            kca_ref[gi, :, dh:2 * dh] = feat_c
            cin_ref[...] = vcr_ref[:, gc]
            vct_ref[gi] = _compress_block(cin_ref, pev_ref, w1v_ref, w2vt_ref, n_units,
                                          transposed=True).astype(BF16)
        j_loc = lax.broadcasted_iota(jnp.int32, (tq, rows_all), 0)
        t_loc = lax.broadcasted_iota(jnp.int32, (tq, rows_all), 1) & (tq - 1)
        madd_ref[0] = jnp.where(j_loc > t_loc, 0.0, -MASK_BIG)
        madd_ref[1] = jnp.where(j_loc <= t_loc, 0.0, -MASK_BIG)
        cpat_ref[...] = ((lax.broadcasted_iota(jnp.int32, (n_units, rows_all), 1) & (tq - 1))
                         - lax.broadcasted_iota(jnp.int32, (n_units, rows_all), 0) * CMP_STRIDE)

    def stack_heads(q2t, extras):
        return jnp.concatenate(
            [jnp.concatenate([q2t[h], extras[h].astype(BF16)], axis=0) for h in range(hpg)], axis=1)

    def softmax_cols(s):
        e = jnp.exp2(s - jnp.max(s, axis=0, keepdims=True))
        return e.astype(BF16), jnp.sum(e, axis=0, keepdims=True)

    row1 = lax.broadcasted_iota(jnp.int32, (dh, 1), 0)
    row_q = lax.broadcasted_iota(jnp.int32, (dh, tq), 0)
    span = WIN + tq
    gw = hpg * dh

    def before_loop(gi):
        qf = q_ref[:, gi * gw:(gi + 1) * gw]
        q2t = [jnp.transpose(qf[:, h * dh:(h + 1) * dh] * (dh ** -0.5 * LOG2E)).astype(BF16) for h in range(hpg)]
        feat = feat_ref[gi]
        q_w = stack_heads(q2t, [jnp.broadcast_to(feat[:, h:h + 1], (dh, tq)) for h in range(hpg)])

        s_w = jnp.dot(kwa_ref[gi, pl.ds(t0, span), :], q_w, preferred_element_type=F32)
        s_w = jnp.concatenate([s_w[0:tq] + madd_ref[0], s_w[tq:WIN], s_w[WIN:span] + madd_ref[1]], axis=0)
        e_w, l_w = softmax_cols(s_w)
        o_win = jnp.dot(vwp_ref[gi, :, pl.ds(t0, span)], e_w, preferred_element_type=F32) * (1.0 / l_w)

        mask_c = cpat_ref[...] >= (CMP_LEN - 1) - t0
        s_c = jnp.where(mask_c, jnp.dot(kca_ref[gi], q_w, preferred_element_type=F32), NEG_INF)
        e_c = jnp.where(mask_c, jnp.exp2(s_c - jnp.max(s_c, axis=0, keepdims=True)), 0.0)
        l_c = jnp.sum(e_c, axis=0, keepdims=True)
        p_c = e_c * (1.0 / jnp.where(l_c > 0.0, l_c, 1.0))
        o_cmp = jnp.dot(vct_ref[gi], p_c.astype(BF16), preferred_element_type=F32)

        p_sum = p_c[:, 0:tq]
        for h in range(1, hpg):
            p_sum = p_sum + p_c[:, h * tq:(h + 1) * tq]
        jn = lax.broadcasted_iota(jnp.int32, (n_slc, n_units), 0) * SEL_LEN
        cn = lax.broadcasted_iota(jnp.int32, (n_slc, n_units), 1) * CMP_STRIDE
        ov_t = jnp.where(cn <= jn + (SEL_LEN - 1),
                         jnp.where(cn + (CMP_LEN - 1) >= jn, jnp.where(cn < n_cmp * CMP_STRIDE, 1.0, 0.0), 0.0),
                         0.0).astype(BF16)
        p_hi = p_sum.astype(BF16)
        p_lo = (p_sum - p_hi.astype(F32)).astype(BF16)
        imp_t = (jnp.dot(ov_t, p_hi, preferred_element_type=F32)
                 + jnp.dot(ov_t, p_lo, preferred_element_type=F32))
        blk = lax.broadcasted_iota(jnp.int32, (n_slc, tq), 0)
        cur = lax.shift_right_logical(t0 + lax.broadcasted_iota(jnp.int32, (n_slc, tq), 1), 6)
        forced = jnp.where(blk == 0, 1, jnp.where(blk == cur, 1, jnp.where(blk == cur - 1, 1, 0)))
        score = jnp.where(forced > 0, FORCE_SCORE, jnp.where(blk <= cur, imp_t, NEG_INF))
        score_ref[gi] = score
        rank = jnp.zeros((n_slc, tq), jnp.int32)
        for j in range(n_slc):
            sj = score_ref[gi, pl.ds(j, 1), :]
            rank = rank + jnp.where(sj > score, 1, jnp.where(blk > j, jnp.where(sj == score, 1, 0), 0))
        sel_bias_t = jnp.where(rank < SEL_TOPK, 0.0, -MASK_BIG)
        sel_bias = jnp.concatenate([sel_bias_t, jnp.zeros((dh - n_slc, tq), F32)], axis=0)

        feat_np = jnp.where(row1 == LANE_PAD, 0.0, feat)
        chunk_col = jnp.where(row1 >= LANE_CHUNK,
                              jnp.where(row1 < LANE_CHUNK + seq // tq,
                                        jnp.where(row1 - LANE_CHUNK < qi, 0.0, -MASK_BIG), 0.0), 0.0)
        extra_d = [jnp.where(row_q < n_slc, sel_bias, feat_np[:, h:h + 1]) for h in range(hpg)]
        q_d = stack_heads(q2t, extra_d)
        q_s = stack_heads(q2t, [x + chunk_col for x in extra_d])
        s_d = jnp.dot(ksa_ref[gi, pl.ds(t0, tq), :], q_d, preferred_element_type=F32) + madd_ref[1]
        m0 = jnp.max(s_d, axis=0, keepdims=True)
        e_d = jnp.exp2(s_d - m0)
        l0 = jnp.sum(e_d, axis=0, keepdims=True)
        acc0 = jnp.dot(vst_ref[gi * dh:(gi + 1) * dh, pl.ds(t0, tq)], e_d.astype(BF16),
                       preferred_element_type=F32)
        return o_win, o_cmp, q_s, (m0, l0, acc0)

    pre = [before_loop(gi) for gi in range(gb)]

    def sel_step(kb, carry):
        k0 = pl.multiple_of(kb * kb_len, kb_len)
        out = []
        for gi in range(gb):
            m, l, acc = carry[gi]
            s = jnp.dot(ksa_ref[gi, pl.ds(k0, kb_len), :], pre[gi][2], preferred_element_type=F32)
            m_new = jnp.maximum(m, jnp.max(s, axis=0, keepdims=True))
            alpha = jnp.exp2(m - m_new)
            e = jnp.exp2(s - m_new)
            l = alpha * l + jnp.sum(e, axis=0, keepdims=True)
            acc = alpha * acc + jnp.dot(vst_ref[gi * dh:(gi + 1) * dh, pl.ds(k0, kb_len)], e.astype(BF16),
                                        preferred_element_type=F32)
            out.append((m_new, l, acc))
        return tuple(out)

    n_kb = lax.shift_right_logical(t0 + kb_len - 1, int(math.log2(kb_len)))
    sel = lax.fori_loop(0, n_kb, sel_step, tuple(p[3] for p in pre))

    for gi in range(gb):
        o_win, o_cmp = pre[gi][0], pre[gi][1]
        _, l_s, acc_s = sel[gi]
        o_sel = acc_s * (1.0 / l_s)
        sg_t = jnp.transpose(jax.nn.sigmoid(gate_ref[:, gi * dh:(gi + 1) * dh]))
        grow = lambda br: jnp.concatenate([sg_t[3 * h + br:3 * h + br + 1, :] for h in range(hpg)], axis=1)
        o = grow(0) * o_cmp + grow(1) * o_sel + grow(2) * o_win
        for h in range(hpg):
            c0 = gi * gw + h * dh
            o_h = jnp.transpose(o[:, h * tq:(h + 1) * tq])
            o_ref[:, c0:c0 + dh] = (o_h * _silu(bg_ref[:, c0:c0 + dh])).astype(o_ref.dtype)


def _slope_features():
    slopes = jnp.asarray(2.0 ** (-8.0 * np.arange(1, B_HEADS + 1) / B_HEADS) * LOG2E, dtype=F32)
    hi = slopes.astype(BF16).astype(F32)
    mid = (slopes - hi).astype(BF16).astype(F32)
    lo = (slopes - hi - mid).astype(BF16).astype(F32)
    cols = jnp.stack([64.0 * hi, hi, 64.0 * mid, mid, 64.0 * lo, lo, jnp.full_like(hi, -MASK_BIG)], axis=1)
    feat = jnp.zeros((B_HEADS, B_DH), F32).at[:, LANE_ALIBI:LANE_PAD + 1].set(cols)
    feat = feat.reshape(B_KV, B_HPG, B_DH)
    return jnp.swapaxes(jnp.pad(feat, ((0, 0), (0, 8 - B_HPG), (0, 0))), 1, 2)


def nsa(p, pt, pvt, pe_k, w1_k, w2_k, pe_v, w1_v, w2t_v, layer, bsz, seq,
        col_q, col_kc, col_vc, col_ks, col_kw, col_bg, col_gate):
    nq = seq // NSA_TQ
    gb = NSA_GB
    gw = B_HPG * B_DH
    n_units = seq // CMP_STRIDE
    rows_all = B_HPG * NSA_TQ
    assert all(c % gb == 0 for c in (col_q, col_kc, col_vc, col_ks, col_kw, col_bg, col_gate)) and B_KV % gb == 0
    tile = lambda c0, w: pl.BlockSpec((NSA_TQ, gb * w), lambda b, g, i: (b * nq + i, c0 // gb + g))
    full = lambda c0: pl.BlockSpec((seq, gb * B_DH), lambda b, g, i: (b, c0 // gb + g))
    full_t = lambda r0: pl.BlockSpec((gb * B_DH, seq), lambda b, g, i: (r0 // gb + g, b))
    wspec = lambda a: pl.BlockSpec((None,) + a.shape[1:], lambda b, g, i: (layer,) + (0,) * (a.ndim - 1))
    return pl.pallas_call(
        functools.partial(_nsa_kernel, seq=seq, gb=gb),
        grid=(bsz, B_KV // gb, nq),
        in_specs=[
            pl.BlockSpec((gb, B_DH, 8), lambda b, g, i: (g, 0, 0)),
            tile(col_q, gw), full(col_kc), full(col_vc), full(col_ks), full(col_kw),
            full_t(0), full_t(B_KV),
            tile(col_bg, gw), tile(col_gate, B_DH),
            wspec(pe_k), wspec(w1_k), wspec(w2_k), wspec(pe_v), wspec(w1_v), wspec(w2t_v),
        ],
        out_specs=pl.BlockSpec((NSA_TQ, gb * gw), lambda b, g, i: (b * nq + i, g)),
        out_shape=jax.ShapeDtypeStruct((bsz * seq, B_HEADS * B_DH), BF16),
        scratch_shapes=[
            pltpu.VMEM((gb, seq, 2 * B_DH), BF16),
            pltpu.VMEM((gb, seq + WIN, 2 * B_DH), BF16),
            pltpu.VMEM((gb, B_DH, seq + WIN), BF16),
            pltpu.VMEM((gb, n_units, 2 * B_DH), BF16),
            pltpu.VMEM((gb, B_DH, n_units), BF16),
            pltpu.VMEM((2, NSA_TQ, rows_all), F32),
            pltpu.VMEM((n_units, rows_all), jnp.int32),
            pltpu.VMEM((gb, seq // SEL_LEN, NSA_TQ), F32),
            pltpu.VMEM((seq, B_DH), F32),
        ],
        compiler_params=_params("parallel", "parallel", "arbitrary"),
        name="nsa_attention",
    )(_slope_features(), p, p, p, p, p, pvt, pvt, pt, pt, pe_k, w1_k, w2_k, pe_v, w1_v, w2t_v)


def _softplus(x):
    return jnp.maximum(x, 0.0) + jnp.log1p(jnp.exp(-jnp.abs(x)))


def _rglru_kernel(xb_ref, g_ref, cw_ref, cb_ref, wa_ref, ba_ref, wi_ref, bi_ref, lam_ref, o_ref,
                  xpad_ref, h_ref, ga_ref, gu_ref, gc_ref):
    tb = RG_TB
    pad = 8

    @pl.when(pl.program_id(2) == 0)
    def _():
        xpad_ref[0:pad, :] = jnp.zeros((pad, RG_BS), F32)
        h_ref[...] = jnp.zeros_like(h_ref)

    x = xb_ref[...]
    xpad_ref[pad:pad + tb, :] = x
    xc = xpad_ref[pl.ds(pad - 3, tb), :] * cw_ref[0:1, :]
    for j in range(1, CONV_W):
        xc = xc + xpad_ref[pl.ds(pad - 3 + j, tb), :] * cw_ref[j:j + 1, :]
    xc = xc + cb_ref[...]
    xpad_ref[0:pad, :] = x[tb - pad:tb]

    xc_b = xc.astype(BF16)
    r = jax.nn.sigmoid(jnp.dot(xc_b, wa_ref[...], preferred_element_type=F32) + ba_ref[...])
    i = jax.nn.sigmoid(jnp.dot(xc_b, wi_ref[...], preferred_element_type=F32) + bi_ref[...])
    log_a = -RG_C * _softplus(-lam_ref[...]) * r
    a = jnp.exp(log_a)
    th = jnp.tanh(log_a)
    u = jnp.sqrt(jnp.maximum(-2.0 * th / (1.0 - th), 0.0)) * (i * xc)

    def scan_rows(a, u, row, n, axis):
        sh = 1
        while sh < n:
            keep = row >= sh
            a_s = jnp.where(keep, pltpu.roll(a, sh, axis), 1.0)
            u_s = jnp.where(keep, pltpu.roll(u, sh, axis), 0.0)
            u = a * u_s + u
            a = a * a_s
            sh *= 2
        return a, u

    grp = 8
    ng = tb // grp
    a, u = scan_rows(a.reshape(ng, grp, RG_BS), u.reshape(ng, grp, RG_BS),
                     lax.broadcasted_iota(jnp.int32, (ng, grp, RG_BS), 1), grp, 1)
    a, u = a.reshape(tb, RG_BS), u.reshape(tb, RG_BS)
    row_g = lax.broadcasted_iota(jnp.int32, (ng, 128), 0)
    h_prev = h_ref[...]
    pieces = []
    for k in range(RG_BS // 128):
        lanes = slice(k * 128, (k + 1) * 128)
        ga_ref[k] = a[:, lanes]
        gu_ref[k] = u[:, lanes]
        a_g, u_g = scan_rows(ga_ref[k, pl.ds(grp - 1, ng, stride=grp), :],
                             gu_ref[k, pl.ds(grp - 1, ng, stride=grp), :], row_g, ng, 0)
        h_end = u_g + a_g * h_prev[:, lanes]
        h_ref[:, lanes] = h_end[ng - 1:ng]
        gc_ref[k] = jnp.where(row_g >= 1, pltpu.roll(h_end, 1, 0), h_prev[:, lanes])
        pieces.append(jnp.concatenate(
            [u[g * grp:(g + 1) * grp, lanes] + a[g * grp:(g + 1) * grp, lanes] * gc_ref[k, pl.ds(g, 1), :]
             for g in range(ng)], axis=0))
    h = jnp.concatenate(pieces, axis=1)
    o_ref[...] = (h * _silu(g_ref[...])).astype(o_ref.dtype)


def rglru(p, conv_w, conv_b, w_a, b_a, w_i, b_i, lam, layer, bsz, seq):
    nt = seq // RG_TB
    d_rnn = RG_BLOCKS * RG_BS
    tile = lambda c0: pl.BlockSpec((RG_TB, RG_BS), lambda b, n, t: (b * nt + t, c0 + n))
    vec = lambda rows: pl.BlockSpec((None, rows, RG_BS), lambda b, n, t: (layer, 0, n))
    mat = pl.BlockSpec((None, None, RG_BS, RG_BS), lambda b, n, t: (layer, n, 0, 0))
    n_layers = conv_b.shape[0]
    row = lambda a: a.reshape(n_layers, 1, d_rnn)
    return pl.pallas_call(
        _rglru_kernel,
        grid=(bsz, RG_BLOCKS, nt),
        in_specs=[tile(0), tile(RG_BLOCKS), vec(CONV_W), vec(1), mat, vec(1), mat, vec(1), vec(1)],
        out_specs=pl.BlockSpec((RG_TB, RG_BS), lambda b, n, t: (b * nt + t, n)),
        out_shape=jax.ShapeDtypeStruct((bsz * seq, d_rnn), BF16),
        scratch_shapes=[pltpu.VMEM((RG_TB + 8, RG_BS), F32), pltpu.VMEM((1, RG_BS), F32),
                        pltpu.VMEM((RG_BS // 128, RG_TB, 128), F32), pltpu.VMEM((RG_BS // 128, RG_TB, 128), F32),
                        pltpu.VMEM((RG_BS // 128, RG_TB // 8, 128), F32)],
        compiler_params=_params("parallel", "parallel", "arbitrary"),
        name="rglru",
    )(p, p, conv_w, row(conv_b), w_a, row(b_a), w_i, row(b_i), row(lam))


def _even_tail_weight(w, sizes):
    start = int(np.sum(sizes[:-2]))
    n_gate, d_b = sizes[-2], sizes[-1]
    gate = w[:, :, start:start + n_gate].reshape(w.shape[0], w.shape[1], B_KV, 3 * B_HPG)
    gate = jnp.pad(gate, ((0, 0), (0, 0), (0, 0), (0, B_DH - 3 * B_HPG))).reshape(w.shape[0], w.shape[1], B_KV * B_DH)
    return jnp.concatenate([w[:, :, start + n_gate:start + n_gate + d_b], gate], axis=2).astype(BF16)


def kernel(x, norm_w, final_norm_w, even_w_in, even_w_out, hgrn_lb_logits, hgrn_norm_w, cmp_pe_k, cmp_w1_k, cmp_w2_k, cmp_pe_v, cmp_w1_v, cmp_w2_v, odd_w_in, odd_w_out, rg_conv_w, rg_conv_b, rg_w_a, rg_b_a, rg_w_i, rg_b_i, rg_lambda):
    bsz, seq, d_model = x.shape
    depth = norm_w.shape[0]
    d_a = A_HEADS * A_DV
    d_b = B_HEADS * B_DH
    d_rnn = RG_BLOCKS * RG_BS
    a_qk = A_HEADS * A_DK
    kvw = B_KV * B_DH
    gw = B_HPG * B_DH
    assert seq % NSA_KB == 0 and seq % RG_TB == 0 and seq % HG_CHUNK == 0 and CMP_LEN == 2 * CMP_STRIDE
    assert seq // SEL_LEN <= LANE_ALIBI and seq // NSA_TQ <= B_DH - LANE_CHUNK and seq >= WIN + NSA_TQ
    sizes = (a_qk, a_qk, d_a, d_a, d_b, kvw, kvw, kvw, kvw, kvw, kvw, 3 * B_HEADS, d_b)
    names = ("a_q", "a_f", "a_i", "a_g", "b_q", "b_kc", "b_vc", "b_ks", "b_vs", "b_kw", "b_vw", "b_gate", "b_g")
    off = dict(zip(names, np.concatenate([[0], np.cumsum(sizes)[:-1]]).tolist()))
    tn = 512
    assert all(off[k] % tn == 0 for k in ("b_vs", "b_kw", "b_vw")) and kvw == tn
    skip_tile = off["b_vs"] // tn
    n_main_tiles = off["b_vw"] // tn - 1
    col_kw = (off["b_kw"] - tn) // B_DH

    even_tail_b = _even_tail_weight(even_w_in, sizes)
    vcols = lambda name: jnp.swapaxes(even_w_in[:, :, off[name]:off[name] + kvw], 1, 2)
    even_vt_b = jnp.concatenate([vcols("b_vs"), vcols("b_vw")], axis=1).astype(BF16)
    w_a_b, w_i_b = rg_w_a.astype(BF16), rg_w_i.astype(BF16)
    w1k_b, w2k_b = cmp_w1_k.astype(BF16), cmp_w2_k.astype(BF16)
    w1v_b, w2tv_b = cmp_w1_v.astype(BF16), jnp.swapaxes(cmp_w2_v, 1, 2).astype(BF16)

    xf = x.reshape(bsz * seq, d_model)
    for layer in range(depth):
        if layer % 2 == 0:
            e = layer // 2
            p = norm_matmul(xf, norm_w[layer], even_w_in, e, n_main_tiles, skip_tile)
            pt = norm_matmul(xf, norm_w[layer], even_tail_b, e, even_tail_b.shape[2] // tn)
            pvt = norm_matmul_t(xf, norm_w[layer], even_vt_b, e)
            ya = hgrn2(p, hgrn_lb_logits, hgrn_norm_w[e], e, bsz, seq,
                       off["a_q"] // A_DK, off["a_f"] // A_DK, off["a_i"] // A_DV, off["a_g"] // A_DV)
            yb = nsa(p, pt, pvt, cmp_pe_k, w1k_b, w2k_b, cmp_pe_v, w1v_b, w2tv_b, e, bsz, seq,
                     off["b_q"] // gw, off["b_kc"] // B_DH, off["b_vc"] // B_DH, off["b_ks"] // B_DH,
                     col_kw, 0, d_b // B_DH)
            xf = proj_residual([ya, yb], even_w_out, e, xf)
        else:
            o = layer // 2
            p = norm_matmul(xf, norm_w[layer], odd_w_in, o, 2 * d_rnn // tn)
            hr = rglru(p, rg_conv_w, rg_conv_b, w_a_b, rg_b_a, w_i_b, rg_b_i, rg_lambda, o, bsz, seq)
            xf = proj_residual([hr], odd_w_out, o, xf)
    return rmsnorm(xf, final_norm_w).reshape(bsz, seq, d_model)
```

```python
import functools
import math

import jax
import jax.numpy as jnp
import numpy as np
from jax import lax
from jax.experimental import pallas as pl
from jax.experimental.pallas import tpu as pltpu

F32 = jnp.float32
BF16 = jnp.bfloat16

NORM_EPS = 1e-6
NEG_INF = -1e30
FORCE_SCORE = 1e30
TINY = 1e-30
LOG2E = 1.4426950408889634

A_HEADS = 16
A_DK = 128
A_DV = 128
HG_CHUNK = 64
HG_SUB = 8

B_HEADS = 16
B_DH = 128
B_KV = 4
B_HPG = B_HEADS // B_KV
CMP_LEN = 32
CMP_STRIDE = 16
SEL_LEN = 64
SEL_TOPK = 16
WIN = 512
NSA_TQ = 128
NSA_KB = 512
NSA_GB = 2
MASK_BIG = 32768.0

LANE_ALIBI = 32
LANE_PAD = 38
LANE_CHUNK = 40

RG_BLOCKS = 10
RG_BS = 256
CONV_W = 4
RG_C = 8.0
RG_TB = 256

VMEM_LIMIT = 56 * 1024 * 1024

_NT = (((1,), (1,)), ((), ()))
_TN = (((0,), (0,)), ((), ()))


def _silu(x):
    return x * jax.nn.sigmoid(x)


def _params(*sem):
    return pltpu.CompilerParams(dimension_semantics=sem, vmem_limit_bytes=VMEM_LIMIT)


def _norm_matmul_kernel(x_ref, nw_ref, w_ref, o_ref, h_ref):
    @pl.when(pl.program_id(1) == 0)
    def _():
        x = x_ref[...]
        ms = jnp.mean(x * x, axis=-1, keepdims=True)
        h_ref[...] = (x * lax.rsqrt(ms + NORM_EPS) * nw_ref[...]).astype(BF16)

    o_ref[...] = jnp.dot(h_ref[...], w_ref[...], preferred_element_type=F32)


def norm_matmul(x, nw, w, layer, n_tiles, skip_tile=None, tm=1024, tn=512):
    m, d = x.shape
    src = (lambda j: j) if skip_tile is None else (lambda j: j + j // skip_tile)
    assert skip_tile is None or n_tiles <= 2 * skip_tile
    return pl.pallas_call(
        _norm_matmul_kernel,
        grid=(m // tm, n_tiles),
        in_specs=[
            pl.BlockSpec((tm, d), lambda i, j: (i, 0)),
            pl.BlockSpec((1, d), lambda i, j: (0, 0)),
            pl.BlockSpec((None, d, tn), lambda i, j: (layer, 0, src(j))),
        ],
        out_specs=pl.BlockSpec((tm, tn), lambda i, j: (i, j)),
        out_shape=jax.ShapeDtypeStruct((m, n_tiles * tn), F32),
        scratch_shapes=[pltpu.VMEM((tm, d), BF16)],
        compiler_params=_params("parallel", "arbitrary"),
        name="norm_matmul",
    )(x, nw.reshape(1, d), w)


def _norm_matmul_t_kernel(x_ref, nw_ref, wt_ref, o_ref, h_ref):
    @pl.when(pl.program_id(1) == 0)
    def _():
        x = x_ref[...]
        ms = jnp.mean(x * x, axis=-1, keepdims=True)
        h_ref[...] = (x * lax.rsqrt(ms + NORM_EPS) * nw_ref[...]).astype(BF16)

    o_ref[...] = lax.dot_general(wt_ref[...], h_ref[...], _NT, preferred_element_type=F32).astype(o_ref.dtype)


def norm_matmul_t(x, nw, wt, layer, tm=1024, tn=512):
    m, d = x.shape
    n = wt.shape[1]
    return pl.pallas_call(
        _norm_matmul_t_kernel,
        grid=(m // tm, n // tn),
        in_specs=[
            pl.BlockSpec((tm, d), lambda i, j: (i, 0)),
            pl.BlockSpec((1, d), lambda i, j: (0, 0)),
            pl.BlockSpec((None, tn, d), lambda i, j: (layer, j, 0)),
        ],
        out_specs=pl.BlockSpec((tn, tm), lambda i, j: (j, i)),
        out_shape=jax.ShapeDtypeStruct((n, m), BF16),
        scratch_shapes=[pltpu.VMEM((tm, d), BF16)],
        compiler_params=_params("parallel", "arbitrary"),
        name="norm_matmul_t",
    )(x, nw.reshape(1, d), wt)


def _proj_residual_kernel(*refs, n_in, with_norm):
    a_refs, w_refs = refs[:n_in], refs[n_in:2 * n_in]
    x_ref = refs[2 * n_in]
    o_ref = refs[-1]
    y = jnp.dot(a_refs[0][...], w_refs[0][...], preferred_element_type=F32)
    for a_ref, w_ref in zip(a_refs[1:], w_refs[1:]):
        y = y + jnp.dot(a_ref[...], w_ref[...], preferred_element_type=F32)
    y = x_ref[...] + y
    if with_norm:
        y = y * lax.rsqrt(jnp.mean(y * y, axis=-1, keepdims=True) + NORM_EPS) * refs[2 * n_in + 1][...]
    o_ref[...] = y


def proj_residual(acts, w, layer, x, tm=1024, tn=512, norm_w=None):
    m, n = x.shape
    n_in = len(acts)
    k = acts[0].shape[1]
    assert norm_w is None or tn == n
    in_specs = [pl.BlockSpec((tm, k), lambda i, j: (i, 0)) for _ in acts]
    in_specs += [pl.BlockSpec((None, k, tn), functools.partial(lambda i, j, s: (layer, s, j), s=s))
                 for s in range(n_in)]
    in_specs += [pl.BlockSpec((tm, tn), lambda i, j: (i, j))]
    extra = []
    if norm_w is not None:
        in_specs += [pl.BlockSpec((1, n), lambda i, j: (0, 0))]
        extra = [norm_w.reshape(1, n)]
    return pl.pallas_call(
        functools.partial(_proj_residual_kernel, n_in=n_in, with_norm=norm_w is not None),
        grid=(m // tm, n // tn),
        in_specs=in_specs,
        out_specs=pl.BlockSpec((tm, tn), lambda i, j: (i, j)),
        out_shape=jax.ShapeDtypeStruct((m, n), F32),
        compiler_params=_params("parallel", "arbitrary"),
        name="proj_residual",
    )(*acts, *([w] * n_in), x, *extra)


def _rmsnorm_kernel(x_ref, w_ref, o_ref):
    x = x_ref[...]
    ms = jnp.mean(x * x, axis=-1, keepdims=True)
    o_ref[...] = x * lax.rsqrt(ms + NORM_EPS) * w_ref[...]


def rmsnorm(x, w, tm=512):
    m, d = x.shape
    return pl.pallas_call(
        _rmsnorm_kernel,
        grid=(m // tm,),
        in_specs=[pl.BlockSpec((tm, d), lambda i: (i, 0)), pl.BlockSpec((1, d), lambda i: (0, 0))],
        out_specs=pl.BlockSpec((tm, d), lambda i: (i, 0)),
        out_shape=jax.ShapeDtypeStruct((m, d), F32),
        compiler_params=_params("parallel"),
        name="final_rmsnorm",
    )(x, w.reshape(1, d))


def _split3(x):
    hi = x.astype(BF16)
    r1 = x - hi.astype(F32)
    mid = r1.astype(BF16)
    lo = (r1 - mid.astype(F32)).astype(BF16)
    return hi, mid, lo


def _hgrn2_levels(c_len, sub):
    levels, seg, half = [], 0, sub
    while half < c_len:
        levels.append((half, seg))
        seg += c_len // (2 * half)
        half *= 2
    return levels, seg


def _hgrn2_kernel(lbl_ref, gain_ref, q_ref, f_ref, v_ref, g_ref, o_ref, st_ref, b_ref, k_ref, vc_ref, *,
                  layer, hb, tblk):
    c_len, sub = HG_CHUNK, HG_SUB
    n_sub = c_len // sub
    levels, n_seg = _hgrn2_levels(c_len, sub)
    lg = lbl_ref[...]
    ex = jnp.exp(lg - jnp.max(lg, axis=0, keepdims=True))
    sm = ex / jnp.sum(ex, axis=0, keepdims=True)
    lb_all = sm[0:1]
    for j in range(1, layer + 1):
        lb_all = lb_all + sm[j:j + 1]
    lb_all = lb_all - sm[0:1]
    gain_all = gain_ref[...]

    @pl.when(pl.program_id(2) == 0)
    def _():
        st_ref[...] = jnp.zeros_like(st_ref)

    tri = jnp.where(lax.broadcasted_iota(jnp.int32, (c_len, c_len), 0)
                    >= lax.broadcasted_iota(jnp.int32, (c_len, c_len), 1), 1.0, 0.0).astype(BF16)
    ones_kk = jnp.ones((A_DK, A_DK), BF16)
    row_sub = lax.broadcasted_iota(jnp.int32, (sub, A_DK), 0)
    zero_blk = jnp.zeros((sub, A_DK), F32)

    def head_chunk(h, r0):
        rows = pl.ds(r0, c_len)
        cols = slice(h * A_DK, (h + 1) * A_DK)
        lb = lb_all[:, cols]
        one_m_lb = 1.0 - lb
        q = _silu(q_ref[rows, cols])
        fz = f_ref[rows, cols]
        v = v_ref[rows, cols]
        vc_ref[h] = v
        v_b = v.astype(BF16)
        f = lb + one_m_lb * jax.nn.sigmoid(fz)
        hi, mid, lo = _split3(jnp.log2(jnp.maximum(f, TINY)))
        kk = one_m_lb * jax.nn.sigmoid(-fz)
        b = (jnp.dot(tri, hi, preferred_element_type=F32)
             + jnp.dot(tri, mid, preferred_element_type=F32)
             + jnp.dot(tri, lo, preferred_element_type=F32))
        b_ref[h] = b
        k_ref[h] = kk
        st = st_ref[h]
        o_inter = lax.dot_general((q * jnp.exp2(b)).astype(BF16), st.astype(BF16), _NT,
                                  preferred_element_type=F32)

        es = []
        for j in range(n_sub):
            bj = b[j * sub:(j + 1) * sub]
            qj = q[j * sub:(j + 1) * sub]
            for s in range(sub):
                bs = b_ref[h, pl.ds(j * sub + s, 1), :]
                ks = k_ref[h, pl.ds(j * sub + s, 1), :]
                es.append(qj * jnp.exp2(jnp.where(row_sub >= s, bj - bs, NEG_INF)) * ks)
        att_d = jnp.dot(jnp.concatenate(es, axis=0).astype(BF16), ones_kk, preferred_element_type=F32)
        o_diag = []
        for j in range(n_sub):
            acc = None
            for s in range(sub):
                u = j * sub + s
                term = att_d[u * sub:(u + 1) * sub] * vc_ref[h, pl.ds(u, 1), :]
                acc = term if acc is None else acc + term
            o_diag.append(acc)

        q_rows, k_rows = [], []
        for j in range(n_sub):
            bj = b[j * sub:(j + 1) * sub]
            q_seg = [zero_blk] * n_seg
            k_seg = [zero_blk] * n_seg
            for half, seg0 in levels:
                blk = (j * sub) // (2 * half)
                m_row = b_ref[h, pl.ds(blk * 2 * half + half - 1, 1), :]
                if ((j * sub) // half) % 2 == 1:
                    q_seg[seg0 + blk] = q[j * sub:(j + 1) * sub] * jnp.exp2(bj - m_row)
                else:
                    k_seg[seg0 + blk] = kk[j * sub:(j + 1) * sub] * jnp.exp2(m_row - bj)
            q_rows.append(jnp.concatenate(q_seg, axis=1))
            k_rows.append(jnp.concatenate(k_seg, axis=1))
        att_o = lax.dot_general(jnp.concatenate(q_rows, axis=0).astype(BF16),
                                jnp.concatenate(k_rows, axis=0).astype(BF16), _NT,
                                preferred_element_type=F32)
        o = (o_inter + jnp.dot(att_o.astype(BF16), v_b, preferred_element_type=F32)
             + jnp.concatenate(o_diag, axis=0))

        b_last = b_ref[h, pl.ds(c_len - 1, 1), :]
        k_dec = (kk * jnp.exp2(b_last - b)).astype(BF16)
        st_ref[h] = st * jnp.exp2(b_last) + lax.dot_general(v_b, k_dec, _TN, preferred_element_type=F32)

        o = o * lax.rsqrt(jnp.mean(o * o, axis=-1, keepdims=True) + NORM_EPS) * gain_all[:, cols]
        o_ref[rows, cols] = (o * _silu(g_ref[rows, cols])).astype(o_ref.dtype)

    def chunk(c, carry):
        r0 = pl.multiple_of(c * c_len, c_len)
        for h in range(hb):
            head_chunk(h, r0)
        return carry

    lax.fori_loop(0, tblk // c_len, chunk, 0)


def hgrn2(p, lb_logits, gain, layer, bsz, seq, col_q, col_f, col_v, col_g, hb=16, tblk=128):
    n_layers = lb_logits.shape[0]
    nt = seq // tblk
    w = hb * A_DK
    blk = lambda c0: pl.BlockSpec((tblk, w), lambda b, h, t: (b * nt + t, c0 // hb + h))
    return pl.pallas_call(
        functools.partial(_hgrn2_kernel, layer=layer, hb=hb, tblk=tblk),
        grid=(bsz, A_HEADS // hb, nt),
        in_specs=[
            pl.BlockSpec((n_layers, w), lambda b, h, t: (0, h)),
            pl.BlockSpec((1, w), lambda b, h, t: (0, h)),
            blk(col_q), blk(col_f), blk(col_v), blk(col_g),
        ],
        out_specs=pl.BlockSpec((tblk, w), lambda b, h, t: (b * nt + t, h)),
        out_shape=jax.ShapeDtypeStruct((bsz * seq, A_HEADS * A_DV), BF16),
        scratch_shapes=[pltpu.VMEM((hb, A_DV, A_DK), F32), pltpu.VMEM((hb, HG_CHUNK, A_DK), F32),
                        pltpu.VMEM((hb, HG_CHUNK, A_DK), F32), pltpu.VMEM((hb, HG_CHUNK, A_DV), F32)],
        compiler_params=_params("parallel", "parallel", "arbitrary"),
        name="hgrn2",
    )(lb_logits, gain.reshape(1, -1), p, p, p, p)


def _position_features(pos, lane):
    ab = jnp.where((lane & 1) == 0, lax.shift_right_logical(pos, 6), pos & 63)
    return jnp.where(lane >= LANE_ALIBI, jnp.where(lane < LANE_PAD, ab, 0), 0)


def _compress_block(a_ref, pe_ref, w1_ref, w2_ref, n_units, transposed=False):
    h_a = jnp.zeros((n_units, B_DH), F32)
    h_b = jnp.zeros((n_units, B_DH), F32)
    for r in range(CMP_STRIDE):
        a_r = a_ref[pl.ds(r, n_units, stride=CMP_STRIDE), :]
        h_a = h_a + jnp.dot((a_r + pe_ref[r:r + 1, :]).astype(BF16), w1_ref[r * B_DH:(r + 1) * B_DH, :],
                            preferred_element_type=F32)
        r2 = CMP_STRIDE + r
        h_b = h_b + jnp.dot((a_r + pe_ref[r2:r2 + 1, :]).astype(BF16), w1_ref[r2 * B_DH:(r2 + 1) * B_DH, :],
                            preferred_element_type=F32)
    hid = _silu(h_a + pltpu.roll(h_b, n_units - 1, 0)).astype(BF16)
    if transposed:
        return lax.dot_general(w2_ref[...], hid, _NT, preferred_element_type=F32)
    return jnp.dot(hid, w2_ref[...], preferred_element_type=F32)


def _nsa_kernel(feat_ref, q_ref, kcr_ref, vcr_ref, ks_ref, kw_ref, vst_ref, vwt_ref, bg_ref, gate_ref,
                pek_ref, w1k_ref, w2k_ref, pev_ref, w1v_ref, w2vt_ref,
                o_ref, ksa_ref, kwa_ref, vwp_ref, kca_ref, vct_ref, madd_ref, cpat_ref, score_ref, cin_ref, *,
                seq, gb):
    tq, kb_len, hpg, dh = NSA_TQ, NSA_KB, B_HPG, B_DH
    rows_all = hpg * tq
    n_slc = seq // SEL_LEN
    n_units = seq // CMP_STRIDE
    n_cmp = (seq - CMP_LEN) // CMP_STRIDE + 1
    qi = pl.program_id(2)
    t0 = pl.multiple_of(qi * tq, tq)

    @pl.when(qi == 0)
    def _():
        pos = lax.broadcasted_iota(jnp.int32, (seq, dh), 0)
        lane = lax.broadcasted_iota(jnp.int32, (seq, dh), 1)
        alibi = _position_features(pos, lane)
        blk_hot = jnp.where(lane == lax.shift_right_logical(pos, 6), 1, 0)
        chunk_hot = jnp.where(lane - LANE_CHUNK == lax.shift_right_logical(pos, 7), 1, 0)
        feat_s = (alibi + blk_hot + chunk_hot).astype(F32).astype(BF16)
        feat_w = alibi.astype(F32).astype(BF16)
        lane_w = lax.broadcasted_iota(jnp.int32, (WIN, dh), 1)
        pad_w = jnp.where(lane_w == LANE_PAD, 1.0, 0.0).astype(BF16)
        c_end = lax.broadcasted_iota(jnp.int32, (n_units, dh), 0) * CMP_STRIDE + (CMP_LEN - 1)
        lane_c = lax.broadcasted_iota(jnp.int32, (n_units, dh), 1)
        feat_c = _position_features(c_end, lane_c).astype(F32).astype(BF16)
        for gi in range(gb):
            gc = slice(gi * dh, (gi + 1) * dh)
            ksa_ref[gi, :, 0:dh] = ks_ref[:, gc].astype(BF16)
            ksa_ref[gi, :, dh:2 * dh] = feat_s
            kwa_ref[gi, WIN:WIN + seq, 0:dh] = kw_ref[:, gc].astype(BF16)
            kwa_ref[gi, WIN:WIN + seq, dh:2 * dh] = feat_w
            kwa_ref[gi, 0:WIN, 0:dh] = jnp.zeros((WIN, dh), BF16)
            kwa_ref[gi, 0:WIN, dh:2 * dh] = pad_w
            vwp_ref[gi, :, 0:WIN] = jnp.zeros((dh, WIN), BF16)
            vwp_ref[gi, :, WIN:WIN + seq] = vwt_ref[gc, :]
            cin_ref[...] = kcr_ref[:, gc]
            kca_ref[gi, :, 0:dh] = _compress_block(cin_ref, pek_ref, w1k_ref, w2k_ref, n_units).astype(BF16)
            kca_ref[gi, :, dh:2 * dh] = feat_c
            cin_ref[...] = vcr_ref[:, gc]
            vct_ref[gi] = _compress_block(cin_ref, pev_ref, w1v_ref, w2vt_ref, n_units,
                                          transposed=True).astype(BF16)
        j_loc = lax.broadcasted_iota(jnp.int32, (tq, rows_all), 0)
        t_loc = lax.broadcasted_iota(jnp.int32, (tq, rows_all), 1) & (tq - 1)
        madd_ref[0] = jnp.where(j_loc > t_loc, 0.0, -MASK_BIG)
        madd_ref[1] = jnp.where(j_loc <= t_loc, 0.0, -MASK_BIG)
        cpat_ref[...] = ((lax.broadcasted_iota(jnp.int32, (n_units, rows_all), 1) & (tq - 1))
                         - lax.broadcasted_iota(jnp.int32, (n_units, rows_all), 0) * CMP_STRIDE)

    def stack_heads(q2t, extras):
        return jnp.concatenate(
            [jnp.concatenate([q2t[h], extras[h].astype(BF16)], axis=0) for h in range(hpg)], axis=1)

    def softmax_cols(s):
        e = jnp.exp2(s - jnp.max(s, axis=0, keepdims=True))
        return e.astype(BF16), jnp.sum(e, axis=0, keepdims=True)

    row1 = lax.broadcasted_iota(jnp.int32, (dh, 1), 0)
    row_q = lax.broadcasted_iota(jnp.int32, (dh, tq), 0)
    span = WIN + tq
    gw = hpg * dh

    def before_loop(gi):
        qf = q_ref[:, gi * gw:(gi + 1) * gw]
        q2t = [jnp.transpose(qf[:, h * dh:(h + 1) * dh] * (dh ** -0.5 * LOG2E)).astype(BF16) for h in range(hpg)]
        feat = feat_ref[gi]
        q_w = stack_heads(q2t, [jnp.broadcast_to(feat[:, h:h + 1], (dh, tq)) for h in range(hpg)])

        s_w = jnp.dot(kwa_ref[gi, pl.ds(t0, span), :], q_w, preferred_element_type=F32)
        s_w = jnp.concatenate([s_w[0:tq] + madd_ref[0], s_w[tq:WIN], s_w[WIN:span] + madd_ref[1]], axis=0)
        e_w, l_w = softmax_cols(s_w)
        o_win = jnp.dot(vwp_ref[gi, :, pl.ds(t0, span)], e_w, preferred_element_type=F32) * (1.0 / l_w)

        mask_c = cpat_ref[...] >= (CMP_LEN - 1) - t0
        s_c = jnp.where(mask_c, jnp.dot(kca_ref[gi], q_w, preferred_element_type=F32), NEG_INF)
        e_c = jnp.where(mask_c, jnp.exp2(s_c - jnp.max(s_c, axis=0, keepdims=True)), 0.0)
        l_c = jnp.sum(e_c, axis=0, keepdims=True)
        p_c = e_c * (1.0 / jnp.where(l_c > 0.0, l_c, 1.0))
        o_cmp = jnp.dot(vct_ref[gi], p_c.astype(BF16), preferred_element_type=F32)

        p_sum = p_c[:, 0:tq]
        for h in range(1, hpg):
            p_sum = p_sum + p_c[:, h * tq:(h + 1) * tq]
        jn = lax.broadcasted_iota(jnp.int32, (n_slc, n_units), 0) * SEL_LEN
        cn = lax.broadcasted_iota(jnp.int32, (n_slc, n_units), 1) * CMP_STRIDE
        ov_t = jnp.where(cn <= jn + (SEL_LEN - 1),
                         jnp.where(cn + (CMP_LEN - 1) >= jn, jnp.where(cn < n_cmp * CMP_STRIDE, 1.0, 0.0), 0.0),
                         0.0).astype(BF16)
        p_hi = p_sum.astype(BF16)
        p_lo = (p_sum - p_hi.astype(F32)).astype(BF16)
        imp_t = (jnp.dot(ov_t, p_hi, preferred_element_type=F32)
                 + jnp.dot(ov_t, p_lo, preferred_element_type=F32))
        blk = lax.broadcasted_iota(jnp.int32, (n_slc, tq), 0)
        cur = lax.shift_right_logical(t0 + lax.broadcasted_iota(jnp.int32, (n_slc, tq), 1), 6)
        forced = jnp.where(blk == 0, 1, jnp.where(blk == cur, 1, jnp.where(blk == cur - 1, 1, 0)))
        score = jnp.where(forced > 0, FORCE_SCORE, jnp.where(blk <= cur, imp_t, NEG_INF))
        score_ref[gi] = score
        rank = jnp.zeros((n_slc, tq), jnp.int32)
        for j in range(n_slc):
            sj = score_ref[gi, pl.ds(j, 1), :]
            rank = rank + jnp.where(sj > score, 1, jnp.where(blk > j, jnp.where(sj == score, 1, 0), 0))
        sel_bias_t = jnp.where(rank < SEL_TOPK, 0.0, -MASK_BIG)
        sel_bias = jnp.concatenate([sel_bias_t, jnp.zeros((dh - n_slc, tq), F32)], axis=0)

        feat_np = jnp.where(row1 == LANE_PAD, 0.0, feat)
        chunk_col = jnp.where(row1 >= LANE_CHUNK,
                              jnp.where(row1 < LANE_CHUNK + seq // tq,
                                        jnp.where(row1 - LANE_CHUNK < qi, 0.0, -MASK_BIG), 0.0), 0.0)
        extra_d = [jnp.where(row_q < n_slc, sel_bias, feat_np[:, h:h + 1]) for h in range(hpg)]
        q_d = stack_heads(q2t, extra_d)
        q_s = stack_heads(q2t, [x + chunk_col for x in extra_d])
        s_d = jnp.dot(ksa_ref[gi, pl.ds(t0, tq), :], q_d, preferred_element_type=F32) + madd_ref[1]
        m0 = jnp.max(s_d, axis=0, keepdims=True)
        e_d = jnp.exp2(s_d - m0)
        l0 = jnp.sum(e_d, axis=0, keepdims=True)
        acc0 = jnp.dot(vst_ref[gi * dh:(gi + 1) * dh, pl.ds(t0, tq)], e_d.astype(BF16),
                       preferred_element_type=F32)
        return o_win, o_cmp, q_s, (m0, l0, acc0)

    pre = [before_loop(gi) for gi in range(gb)]

    def sel_step(kb, carry):
        k0 = pl.multiple_of(kb * kb_len, kb_len)
        out = []
        for gi in range(gb):
            m, l, acc = carry[gi]
            s = jnp.dot(ksa_ref[gi, pl.ds(k0, kb_len), :], pre[gi][2], preferred_element_type=F32)
            m_new = jnp.maximum(m, jnp.max(s, axis=0, keepdims=True))
            alpha = jnp.exp2(m - m_new)
            e = jnp.exp2(s - m_new)
            l = alpha * l + jnp.sum(e, axis=0, keepdims=True)
            acc = alpha * acc + jnp.dot(vst_ref[gi * dh:(gi + 1) * dh, pl.ds(k0, kb_len)], e.astype(BF16),
                                        preferred_element_type=F32)
            out.append((m_new, l, acc))
        return tuple(out)

    n_kb = lax.shift_right_logical(t0 + kb_len - 1, int(math.log2(kb_len)))
    sel = lax.fori_loop(0, n_kb, sel_step, tuple(p[3] for p in pre))

    for gi in range(gb):
        o_win, o_cmp = pre[gi][0], pre[gi][1]
        _, l_s, acc_s = sel[gi]
        o_sel = acc_s * (1.0 / l_s)
        sg_t = jnp.transpose(jax.nn.sigmoid(gate_ref[:, gi * dh:(gi + 1) * dh]))
        grow = lambda br: jnp.concatenate([sg_t[3 * h + br:3 * h + br + 1, :] for h in range(hpg)], axis=1)
        o = grow(0) * o_cmp + grow(1) * o_sel + grow(2) * o_win
        for h in range(hpg):
            c0 = gi * gw + h * dh
            o_h = jnp.transpose(o[:, h * tq:(h + 1) * tq])
            o_ref[:, c0:c0 + dh] = (o_h * _silu(bg_ref[:, c0:c0 + dh])).astype(o_ref.dtype)


def _slope_features():
    slopes = jnp.asarray(2.0 ** (-8.0 * np.arange(1, B_HEADS + 1) / B_HEADS) * LOG2E, dtype=F32)
    hi = slopes.astype(BF16).astype(F32)
    mid = (slopes - hi).astype(BF16).astype(F32)
    lo = (slopes - hi - mid).astype(BF16).astype(F32)
    cols = jnp.stack([64.0 * hi, hi, 64.0 * mid, mid, 64.0 * lo, lo, jnp.full_like(hi, -MASK_BIG)], axis=1)
    feat = jnp.zeros((B_HEADS, B_DH), F32).at[:, LANE_ALIBI:LANE_PAD + 1].set(cols)
    feat = feat.reshape(B_KV, B_HPG, B_DH)
    return jnp.swapaxes(jnp.pad(feat, ((0, 0), (0, 8 - B_HPG), (0, 0))), 1, 2)


def nsa(p, pt, pvt, pe_k, w1_k, w2_k, pe_v, w1_v, w2t_v, layer, bsz, seq,
        col_q, col_kc, col_vc, col_ks, col_kw, col_bg, col_gate):
    nq = seq // NSA_TQ
    gb = NSA_GB
    gw = B_HPG * B_DH
    n_units = seq // CMP_STRIDE
    rows_all = B_HPG * NSA_TQ
    assert all(c % gb == 0 for c in (col_q, col_kc, col_vc, col_ks, col_kw, col_bg, col_gate)) and B_KV % gb == 0
    tile = lambda c0, w: pl.BlockSpec((NSA_TQ, gb * w), lambda b, g, i: (b * nq + i, c0 // gb + g))
    full = lambda c0: pl.BlockSpec((seq, gb * B_DH), lambda b, g, i: (b, c0 // gb + g))
    full_t = lambda r0: pl.BlockSpec((gb * B_DH, seq), lambda b, g, i: (r0 // gb + g, b))
    wspec = lambda a: pl.BlockSpec((None,) + a.shape[1:], lambda b, g, i: (layer,) + (0,) * (a.ndim - 1))
    return pl.pallas_call(
        functools.partial(_nsa_kernel, seq=seq, gb=gb),
        grid=(bsz, B_KV // gb, nq),
        in_specs=[
            pl.BlockSpec((gb, B_DH, 8), lambda b, g, i: (g, 0, 0)),
            tile(col_q, gw), full(col_kc), full(col_vc), full(col_ks), full(col_kw),
            full_t(0), full_t(B_KV),
            tile(col_bg, gw), tile(col_gate, B_DH),
            wspec(pe_k), wspec(w1_k), wspec(w2_k), wspec(pe_v), wspec(w1_v), wspec(w2t_v),
        ],
        out_specs=pl.BlockSpec((NSA_TQ, gb * gw), lambda b, g, i: (b * nq + i, g)),
        out_shape=jax.ShapeDtypeStruct((bsz * seq, B_HEADS * B_DH), BF16),
        scratch_shapes=[
            pltpu.VMEM((gb, seq, 2 * B_DH), BF16),
            pltpu.VMEM((gb, seq + WIN, 2 * B_DH), BF16),
            pltpu.VMEM((gb, B_DH, seq + WIN), BF16),
            pltpu.VMEM((gb, n_units, 2 * B_DH), BF16),
            pltpu.VMEM((gb, B_DH, n_units), BF16),
            pltpu.VMEM((2, NSA_TQ, rows_all), F32),
            pltpu.VMEM((n_units, rows_all), jnp.int32),
            pltpu.VMEM((gb, seq // SEL_LEN, NSA_TQ), F32),
            pltpu.VMEM((seq, B_DH), F32),
        ],
        compiler_params=_params("parallel", "parallel", "arbitrary"),
        name="nsa_attention",
    )(_slope_features(), p, p, p, p, p, pvt, pvt, pt, pt, pe_k, w1_k, w2_k, pe_v, w1_v, w2t_v)


def _softplus(x):
    return jnp.maximum(x, 0.0) + jnp.log1p(jnp.exp(-jnp.abs(x)))


def _rglru_kernel(xb_ref, g_ref, cw_ref, cb_ref, wa_ref, ba_ref, wi_ref, bi_ref, lam_ref, o_ref,
                  xpad_ref, h_ref, ga_ref, gu_ref, gc_ref):
    tb = RG_TB
    pad = 8

    @pl.when(pl.program_id(2) == 0)
    def _():
        xpad_ref[0:pad, :] = jnp.zeros((pad, RG_BS), F32)
        h_ref[...] = jnp.zeros_like(h_ref)

    x = xb_ref[...]
    xpad_ref[pad:pad + tb, :] = x
    xc = xpad_ref[pl.ds(pad - 3, tb), :] * cw_ref[0:1, :]
    for j in range(1, CONV_W):
        xc = xc + xpad_ref[pl.ds(pad - 3 + j, tb), :] * cw_ref[j:j + 1, :]
    xc = xc + cb_ref[...]
    xpad_ref[0:pad, :] = x[tb - pad:tb]

    xc_b = xc.astype(BF16)
    r = jax.nn.sigmoid(jnp.dot(xc_b, wa_ref[...], preferred_element_type=F32) + ba_ref[...])
    i = jax.nn.sigmoid(jnp.dot(xc_b, wi_ref[...], preferred_element_type=F32) + bi_ref[...])
    log_a = -RG_C * _softplus(-lam_ref[...]) * r
    a = jnp.exp(log_a)
    th = jnp.tanh(log_a)
    u = jnp.sqrt(jnp.maximum(-2.0 * th / (1.0 - th), 0.0)) * (i * xc)

    def scan_rows(a, u, row, n, axis):
        sh = 1
        while sh < n:
            keep = row >= sh
            a_s = jnp.where(keep, pltpu.roll(a, sh, axis), 1.0)
            u_s = jnp.where(keep, pltpu.roll(u, sh, axis), 0.0)
            u = a * u_s + u
            a = a * a_s
            sh *= 2
        return a, u

    grp = 8
    ng = tb // grp
    a, u = scan_rows(a.reshape(ng, grp, RG_BS), u.reshape(ng, grp, RG_BS),
                     lax.broadcasted_iota(jnp.int32, (ng, grp, RG_BS), 1), grp, 1)
    a, u = a.reshape(tb, RG_BS), u.reshape(tb, RG_BS)
    row_g = lax.broadcasted_iota(jnp.int32, (ng, 128), 0)
    h_prev = h_ref[...]
    pieces = []
    for k in range(RG_BS // 128):
        lanes = slice(k * 128, (k + 1) * 128)
        ga_ref[k] = a[:, lanes]
        gu_ref[k] = u[:, lanes]
        a_g, u_g = scan_rows(ga_ref[k, pl.ds(grp - 1, ng, stride=grp), :],
                             gu_ref[k, pl.ds(grp - 1, ng, stride=grp), :], row_g, ng, 0)
        h_end = u_g + a_g * h_prev[:, lanes]
        h_ref[:, lanes] = h_end[ng - 1:ng]
        gc_ref[k] = jnp.where(row_g >= 1, pltpu.roll(h_end, 1, 0), h_prev[:, lanes])
        pieces.append(jnp.concatenate(
            [u[g * grp:(g + 1) * grp, lanes] + a[g * grp:(g + 1) * grp, lanes] * gc_ref[k, pl.ds(g, 1), :]
             for g in range(ng)], axis=0))
    h = jnp.concatenate(pieces, axis=1)
    o_ref[...] = (h * _silu(g_ref[...])).astype(o_ref.dtype)


def rglru(p, conv_w, conv_b, w_a, b_a, w_i, b_i, lam, layer, bsz, seq):
    nt = seq // RG_TB
    d_rnn = RG_BLOCKS * RG_BS
    tile = lambda c0: pl.BlockSpec((RG_TB, RG_BS), lambda b, n, t: (b * nt + t, c0 + n))
    vec = lambda rows: pl.BlockSpec((None, rows, RG_BS), lambda b, n, t: (layer, 0, n))
    mat = pl.BlockSpec((None, None, RG_BS, RG_BS), lambda b, n, t: (layer, n, 0, 0))
    n_layers = conv_b.shape[0]
    row = lambda a: a.reshape(n_layers, 1, d_rnn)
    return pl.pallas_call(
        _rglru_kernel,
        grid=(bsz, RG_BLOCKS, nt),
        in_specs=[tile(0), tile(RG_BLOCKS), vec(CONV_W), vec(1), mat, vec(1), mat, vec(1), vec(1)],
        out_specs=pl.BlockSpec((RG_TB, RG_BS), lambda b, n, t: (b * nt + t, n)),
        out_shape=jax.ShapeDtypeStruct((bsz * seq, d_rnn), BF16),
        scratch_shapes=[pltpu.VMEM((RG_TB + 8, RG_BS), F32), pltpu.VMEM((1, RG_BS), F32),
                        pltpu.VMEM((RG_BS // 128, RG_TB, 128), F32), pltpu.VMEM((RG_BS // 128, RG_TB, 128), F32),
                        pltpu.VMEM((RG_BS // 128, RG_TB // 8, 128), F32)],
        compiler_params=_params("parallel", "parallel", "arbitrary"),
        name="rglru",
    )(p, p, conv_w, row(conv_b), w_a, row(b_a), w_i, row(b_i), row(lam))


def _even_tail_weight(w, sizes):
    start = int(np.sum(sizes[:-2]))
    n_gate, d_b = sizes[-2], sizes[-1]
    gate = w[:, :, start:start + n_gate].reshape(w.shape[0], w.shape[1], B_KV, 3 * B_HPG)
    gate = jnp.pad(gate, ((0, 0), (0, 0), (0, 0), (0, B_DH - 3 * B_HPG))).reshape(w.shape[0], w.shape[1], B_KV * B_DH)
    return jnp.concatenate([w[:, :, start + n_gate:start + n_gate + d_b], gate], axis=2).astype(BF16)


def kernel(x, norm_w, final_norm_w, even_w_in, even_w_out, hgrn_lb_logits, hgrn_norm_w, cmp_pe_k, cmp_w1_k, cmp_w2_k, cmp_pe_v, cmp_w1_v, cmp_w2_v, odd_w_in, odd_w_out, rg_conv_w, rg_conv_b, rg_w_a, rg_b_a, rg_w_i, rg_b_i, rg_lambda):
    bsz, seq, d_model = x.shape
    depth = norm_w.shape[0]
    d_a = A_HEADS * A_DV
    d_b = B_HEADS * B_DH
    d_rnn = RG_BLOCKS * RG_BS
    a_qk = A_HEADS * A_DK
    kvw = B_KV * B_DH
    gw = B_HPG * B_DH
    assert seq % NSA_KB == 0 and seq % RG_TB == 0 and seq % HG_CHUNK == 0 and CMP_LEN == 2 * CMP_STRIDE
    assert seq // SEL_LEN <= LANE_ALIBI and seq // NSA_TQ <= B_DH - LANE_CHUNK and seq >= WIN + NSA_TQ
    sizes = (a_qk, a_qk, d_a, d_a, d_b, kvw, kvw, kvw, kvw, kvw, kvw, 3 * B_HEADS, d_b)
    names = ("a_q", "a_f", "a_i", "a_g", "b_q", "b_kc", "b_vc", "b_ks", "b_vs", "b_kw", "b_vw", "b_gate", "b_g")
    off = dict(zip(names, np.concatenate([[0], np.cumsum(sizes)[:-1]]).tolist()))
    tn = 512
    assert all(off[k] % tn == 0 for k in ("b_vs", "b_kw", "b_vw")) and kvw == tn
    skip_tile = off["b_vs"] // tn
    n_main_tiles = off["b_vw"] // tn - 1
    col_kw = (off["b_kw"] - tn) // B_DH

    even_in_b = even_w_in.astype(BF16)
    even_tail_b = _even_tail_weight(even_w_in, sizes)
    vcols = lambda name: jnp.swapaxes(even_w_in[:, :, off[name]:off[name] + kvw], 1, 2)
    even_vt_b = jnp.concatenate([vcols("b_vs"), vcols("b_vw")], axis=1).astype(BF16)
    even_out_b = even_w_out.astype(BF16)
    odd_in_b = odd_w_in.astype(BF16)
    odd_out_b = odd_w_out.astype(BF16)
    w_a_b, w_i_b = rg_w_a.astype(BF16), rg_w_i.astype(BF16)
    w1k_b, w2k_b = cmp_w1_k.astype(BF16), cmp_w2_k.astype(BF16)
    w1v_b, w2tv_b = cmp_w1_v.astype(BF16), jnp.swapaxes(cmp_w2_v, 1, 2).astype(BF16)

    xf = x.reshape(bsz * seq, d_model)
    for layer in range(depth):
        if layer % 2 == 0:
            e = layer // 2
            p = norm_matmul(xf, norm_w[layer], even_in_b, e, n_main_tiles, skip_tile)
            pt = norm_matmul(xf, norm_w[layer], even_tail_b, e, even_tail_b.shape[2] // tn)
            pvt = norm_matmul_t(xf, norm_w[layer], even_vt_b, e)
            ya = hgrn2(p, hgrn_lb_logits, hgrn_norm_w[e], e, bsz, seq,
                       off["a_q"] // A_DK, off["a_f"] // A_DK, off["a_i"] // A_DV, off["a_g"] // A_DV)
            yb = nsa(p, pt, pvt, cmp_pe_k, w1k_b, w2k_b, cmp_pe_v, w1v_b, w2tv_b, e, bsz, seq,
                     off["b_q"] // gw, off["b_kc"] // B_DH, off["b_vc"] // B_DH, off["b_ks"] // B_DH,
                     col_kw, 0, d_b // B_DH)
            xf = proj_residual([ya, yb], even_out_b, e, xf)
        else:
            o = layer // 2
            p = norm_matmul(xf, norm_w[layer], odd_in_b, o, 2 * d_rnn // tn)
            hr = rglru(p, rg_conv_w, rg_conv_b, w_a_b, rg_b_a, w_i_b, rg_b_i, rg_lambda, o, bsz, seq)
            if layer == depth - 1:
                xf = proj_residual([hr], odd_out_b, o, xf, tm=512, tn=d_model, norm_w=final_norm_w)
            else:
                xf = proj_residual([hr], odd_out_b, o, xf)
    if depth % 2 == 1:
        xf = rmsnorm(xf, final_norm_w)
    return xf.reshape(bsz, seq, d_model)
```

```python
import functools
import math

import jax
import jax.numpy as jnp
import numpy as np
from jax import lax
from jax.experimental import pallas as pl
from jax.experimental.pallas import tpu as pltpu

F32 = jnp.float32
BF16 = jnp.bfloat16

NORM_EPS = 1e-6
NEG_INF = -1e30
FORCE_SCORE = 1e30
TINY = 1e-30
LOG2E = 1.4426950408889634

A_HEADS = 16
A_DK = 128
A_DV = 128
HG_CHUNK = 64
HG_SUB = 8

B_HEADS = 16
B_DH = 128
B_KV = 4
B_HPG = B_HEADS // B_KV
CMP_LEN = 32
CMP_STRIDE = 16
SEL_LEN = 64
SEL_TOPK = 16
WIN = 512
NSA_TQ = 128
NSA_KB = 512
NSA_GB = 2
MASK_BIG = 2.0 ** 100

LANE_ALIBI = 32
LANE_PAD = 38
LANE_CHUNK = 40

RG_BLOCKS = 10
RG_BS = 256
CONV_W = 4
RG_C = 8.0
RG_TB = 256

VMEM_LIMIT = 56 * 1024 * 1024

_NT = (((1,), (1,)), ((), ()))
_TN = (((0,), (0,)), ((), ()))


def _silu(x):
    return x * jax.nn.sigmoid(x)


def _params(*sem):
    return pltpu.CompilerParams(dimension_semantics=sem, vmem_limit_bytes=VMEM_LIMIT)


def _norm_matmul_kernel(x_ref, nw_ref, w_ref, o_ref, h_ref):
    @pl.when(pl.program_id(1) == 0)
    def _():
        x = x_ref[...]
        ms = jnp.mean(x * x, axis=-1, keepdims=True)
        h_ref[...] = (x * lax.rsqrt(ms + NORM_EPS) * nw_ref[...]).astype(BF16)

    o_ref[...] = jnp.dot(h_ref[...], w_ref[...], preferred_element_type=F32)


def norm_matmul(x, nw, w, layer, n_tiles, skip_tile=None, tm=1024, tn=512):
    m, d = x.shape
    src = (lambda j: j) if skip_tile is None else (lambda j: j + j // skip_tile)
    assert skip_tile is None or n_tiles <= 2 * skip_tile
    return pl.pallas_call(
        _norm_matmul_kernel,
        grid=(m // tm, n_tiles),
        in_specs=[
            pl.BlockSpec((tm, d), lambda i, j: (i, 0)),
            pl.BlockSpec((1, d), lambda i, j: (0, 0)),
            pl.BlockSpec((None, d, tn), lambda i, j: (layer, 0, src(j))),
        ],
        out_specs=pl.BlockSpec((tm, tn), lambda i, j: (i, j)),
        out_shape=jax.ShapeDtypeStruct((m, n_tiles * tn), F32),
        scratch_shapes=[pltpu.VMEM((tm, d), BF16)],
        compiler_params=_params("parallel", "arbitrary"),
        name="norm_matmul",
    )(x, nw.reshape(1, d), w)


def _norm_matmul_t_kernel(x_ref, nw_ref, wt_ref, o_ref, h_ref):
    @pl.when(pl.program_id(1) == 0)
    def _():
        x = x_ref[...]
        ms = jnp.mean(x * x, axis=-1, keepdims=True)
        h_ref[...] = (x * lax.rsqrt(ms + NORM_EPS) * nw_ref[...]).astype(BF16)

    o_ref[...] = lax.dot_general(wt_ref[...], h_ref[...], _NT, preferred_element_type=F32).astype(o_ref.dtype)


def norm_matmul_t(x, nw, wt, layer, tm=1024, tn=512):
    m, d = x.shape
    n = wt.shape[1]
    return pl.pallas_call(
        _norm_matmul_t_kernel,
        grid=(m // tm, n // tn),
        in_specs=[
            pl.BlockSpec((tm, d), lambda i, j: (i, 0)),
            pl.BlockSpec((1, d), lambda i, j: (0, 0)),
            pl.BlockSpec((None, tn, d), lambda i, j: (layer, j, 0)),
        ],
        out_specs=pl.BlockSpec((tn, tm), lambda i, j: (j, i)),
        out_shape=jax.ShapeDtypeStruct((n, m), BF16),
        scratch_shapes=[pltpu.VMEM((tm, d), BF16)],
        compiler_params=_params("parallel", "arbitrary"),
        name="norm_matmul_t",
    )(x, nw.reshape(1, d), wt)


def _proj_residual_kernel(*refs, n_in, with_norm):
    a_refs, w_refs = refs[:n_in], refs[n_in:2 * n_in]
    x_ref = refs[2 * n_in]
    o_ref = refs[-1]
    y = jnp.dot(a_refs[0][...], w_refs[0][...], preferred_element_type=F32)
    for a_ref, w_ref in zip(a_refs[1:], w_refs[1:]):
        y = y + jnp.dot(a_ref[...], w_ref[...], preferred_element_type=F32)
    y = x_ref[...] + y
    if with_norm:
        y = y * lax.rsqrt(jnp.mean(y * y, axis=-1, keepdims=True) + NORM_EPS) * refs[2 * n_in + 1][...]
    o_ref[...] = y


def proj_residual(acts, w, layer, x, tm=1024, tn=512, norm_w=None):
    m, n = x.shape
    n_in = len(acts)
    k = acts[0].shape[1]
    assert norm_w is None or tn == n
    in_specs = [pl.BlockSpec((tm, k), lambda i, j: (i, 0)) for _ in acts]
    in_specs += [pl.BlockSpec((None, k, tn), functools.partial(lambda i, j, s: (layer, s, j), s=s))
                 for s in range(n_in)]
    in_specs += [pl.BlockSpec((tm, tn), lambda i, j: (i, j))]
    extra = []
    if norm_w is not None:
        in_specs += [pl.BlockSpec((1, n), lambda i, j: (0, 0))]
        extra = [norm_w.reshape(1, n)]
    return pl.pallas_call(
        functools.partial(_proj_residual_kernel, n_in=n_in, with_norm=norm_w is not None),
        grid=(m // tm, n // tn),
        in_specs=in_specs,
        out_specs=pl.BlockSpec((tm, tn), lambda i, j: (i, j)),
        out_shape=jax.ShapeDtypeStruct((m, n), F32),
        compiler_params=_params("parallel", "arbitrary"),
        name="proj_residual",
    )(*acts, *([w] * n_in), x, *extra)


def _rmsnorm_kernel(x_ref, w_ref, o_ref):
    x = x_ref[...]
    ms = jnp.mean(x * x, axis=-1, keepdims=True)
    o_ref[...] = x * lax.rsqrt(ms + NORM_EPS) * w_ref[...]


def rmsnorm(x, w, tm=512):
    m, d = x.shape
    return pl.pallas_call(
        _rmsnorm_kernel,
        grid=(m // tm,),
        in_specs=[pl.BlockSpec((tm, d), lambda i: (i, 0)), pl.BlockSpec((1, d), lambda i: (0, 0))],
        out_specs=pl.BlockSpec((tm, d), lambda i: (i, 0)),
        out_shape=jax.ShapeDtypeStruct((m, d), F32),
        compiler_params=_params("parallel"),
        name="final_rmsnorm",
    )(x, w.reshape(1, d))


def _split3(x):
    hi = x.astype(BF16)
    r1 = x - hi.astype(F32)
    mid = r1.astype(BF16)
    lo = (r1 - mid.astype(F32)).astype(BF16)
    return hi, mid, lo


def _hgrn2_levels(c_len, sub):
    levels, seg, half = [], 0, sub
    while half < c_len:
        levels.append((half, seg))
        seg += c_len // (2 * half)
        half *= 2
    return levels, seg


def _hgrn2_kernel(lbl_ref, gain_ref, q_ref, f_ref, v_ref, g_ref, o_ref, st_ref, b_ref, k_ref, vc_ref, *,
                  layer, hb, tblk):
    c_len, sub = HG_CHUNK, HG_SUB
    n_sub = c_len // sub
    levels, n_seg = _hgrn2_levels(c_len, sub)
    lg = lbl_ref[...]
    ex = jnp.exp(lg - jnp.max(lg, axis=0, keepdims=True))
    sm = ex / jnp.sum(ex, axis=0, keepdims=True)
    lb_all = sm[0:1]
    for j in range(1, layer + 1):
        lb_all = lb_all + sm[j:j + 1]
    lb_all = lb_all - sm[0:1]
    gain_all = gain_ref[...]

    @pl.when(pl.program_id(2) == 0)
    def _():
        st_ref[...] = jnp.zeros_like(st_ref)

    tri = jnp.where(lax.broadcasted_iota(jnp.int32, (c_len, c_len), 0)
                    >= lax.broadcasted_iota(jnp.int32, (c_len, c_len), 1), 1.0, 0.0).astype(BF16)
    ones_kk = jnp.ones((A_DK, A_DK), BF16)
    row_sub = lax.broadcasted_iota(jnp.int32, (sub, A_DK), 0)
    zero_blk = jnp.zeros((sub, A_DK), F32)

    def head_chunk(h, r0):
        rows = pl.ds(r0, c_len)
        cols = slice(h * A_DK, (h + 1) * A_DK)
        lb = lb_all[:, cols]
        one_m_lb = 1.0 - lb
        q = _silu(q_ref[rows, cols])
        fz = f_ref[rows, cols]
        v = v_ref[rows, cols]
        vc_ref[h] = v
        v_b = v.astype(BF16)
        f = lb + one_m_lb * jax.nn.sigmoid(fz)
        hi, mid, lo = _split3(jnp.log2(jnp.maximum(f, TINY)))
        kk = one_m_lb * jax.nn.sigmoid(-fz)
        b = (jnp.dot(tri, hi, preferred_element_type=F32)
             + jnp.dot(tri, mid, preferred_element_type=F32)
             + jnp.dot(tri, lo, preferred_element_type=F32))
        b_ref[h] = b
        k_ref[h] = kk
        st = st_ref[h]
        o_inter = lax.dot_general((q * jnp.exp2(b)).astype(BF16), st.astype(BF16), _NT,
                                  preferred_element_type=F32)

        es = []
        for j in range(n_sub):
            bj = b[j * sub:(j + 1) * sub]
            qj = q[j * sub:(j + 1) * sub]
            for s in range(sub):
                bs = b_ref[h, pl.ds(j * sub + s, 1), :]
                ks = k_ref[h, pl.ds(j * sub + s, 1), :]
                es.append(qj * jnp.exp2(jnp.where(row_sub >= s, bj - bs, NEG_INF)) * ks)
        att_d = jnp.dot(jnp.concatenate(es, axis=0).astype(BF16), ones_kk, preferred_element_type=F32)
        o_diag = []
        for j in range(n_sub):
            acc = None
            for s in range(sub):
                u = j * sub + s
                term = att_d[u * sub:(u + 1) * sub] * vc_ref[h, pl.ds(u, 1), :]
                acc = term if acc is None else acc + term
            o_diag.append(acc)

        q_rows, k_rows = [], []
        for j in range(n_sub):
            bj = b[j * sub:(j + 1) * sub]
            q_seg = [zero_blk] * n_seg
            k_seg = [zero_blk] * n_seg
            for half, seg0 in levels:
                blk = (j * sub) // (2 * half)
                m_row = b_ref[h, pl.ds(blk * 2 * half + half - 1, 1), :]
                if ((j * sub) // half) % 2 == 1:
                    q_seg[seg0 + blk] = q[j * sub:(j + 1) * sub] * jnp.exp2(bj - m_row)
                else:
                    k_seg[seg0 + blk] = kk[j * sub:(j + 1) * sub] * jnp.exp2(m_row - bj)
            q_rows.append(jnp.concatenate(q_seg, axis=1))
            k_rows.append(jnp.concatenate(k_seg, axis=1))
        att_o = lax.dot_general(jnp.concatenate(q_rows, axis=0).astype(BF16),
                                jnp.concatenate(k_rows, axis=0).astype(BF16), _NT,
                                preferred_element_type=F32)
        o = (o_inter + jnp.dot(att_o.astype(BF16), v_b, preferred_element_type=F32)
             + jnp.concatenate(o_diag, axis=0))

        b_last = b_ref[h, pl.ds(c_len - 1, 1), :]
        k_dec = (kk * jnp.exp2(b_last - b)).astype(BF16)
        st_ref[h] = st * jnp.exp2(b_last) + lax.dot_general(v_b, k_dec, _TN, preferred_element_type=F32)

        o = o * lax.rsqrt(jnp.mean(o * o, axis=-1, keepdims=True) + NORM_EPS) * gain_all[:, cols]
        o_ref[rows, cols] = (o * _silu(g_ref[rows, cols])).astype(o_ref.dtype)

    def chunk(c, carry):
        r0 = pl.multiple_of(c * c_len, c_len)
        for h in range(hb):
            head_chunk(h, r0)
        return carry

    lax.fori_loop(0, tblk // c_len, chunk, 0)


def hgrn2(p, lb_logits, gain, layer, bsz, seq, col_q, col_f, col_v, col_g, hb=16, tblk=128):
    n_layers = lb_logits.shape[0]
    nt = seq // tblk
    w = hb * A_DK
    blk = lambda c0: pl.BlockSpec((tblk, w), lambda b, h, t: (b * nt + t, c0 // hb + h))
    return pl.pallas_call(
        functools.partial(_hgrn2_kernel, layer=layer, hb=hb, tblk=tblk),
        grid=(bsz, A_HEADS // hb, nt),
        in_specs=[
            pl.BlockSpec((n_layers, w), lambda b, h, t: (0, h)),
            pl.BlockSpec((1, w), lambda b, h, t: (0, h)),
            blk(col_q), blk(col_f), blk(col_v), blk(col_g),
        ],
        out_specs=pl.BlockSpec((tblk, w), lambda b, h, t: (b * nt + t, h)),
        out_shape=jax.ShapeDtypeStruct((bsz * seq, A_HEADS * A_DV), BF16),
        scratch_shapes=[pltpu.VMEM((hb, A_DV, A_DK), F32), pltpu.VMEM((hb, HG_CHUNK, A_DK), F32),
                        pltpu.VMEM((hb, HG_CHUNK, A_DK), F32), pltpu.VMEM((hb, HG_CHUNK, A_DV), F32)],
        compiler_params=_params("parallel", "parallel", "arbitrary"),
        name="hgrn2",
    )(lb_logits, gain.reshape(1, -1), p, p, p, p)


def _position_features(pos, lane):
    ab = jnp.where((lane & 1) == 0, lax.shift_right_logical(pos, 6), pos & 63)
    return jnp.where(lane >= LANE_ALIBI, jnp.where(lane < LANE_PAD, ab, 0), 0)


def _compress_block(a_ref, pe_ref, w1_ref, w2_ref, n_units, transposed=False):
    h_a = jnp.zeros((n_units, B_DH), F32)
    h_b = jnp.zeros((n_units, B_DH), F32)
    for r in range(CMP_STRIDE):
        a_r = a_ref[pl.ds(r, n_units, stride=CMP_STRIDE), :]
        h_a = h_a + jnp.dot((a_r + pe_ref[r:r + 1, :]).astype(BF16), w1_ref[r * B_DH:(r + 1) * B_DH, :],
                            preferred_element_type=F32)
        r2 = CMP_STRIDE + r
        h_b = h_b + jnp.dot((a_r + pe_ref[r2:r2 + 1, :]).astype(BF16), w1_ref[r2 * B_DH:(r2 + 1) * B_DH, :],
                            preferred_element_type=F32)
    hid = _silu(h_a + pltpu.roll(h_b, n_units - 1, 0)).astype(BF16)
    if transposed:
        return lax.dot_general(w2_ref[...], hid, _NT, preferred_element_type=F32)
    return jnp.dot(hid, w2_ref[...], preferred_element_type=F32)


def _nsa_kernel(feat_ref, q_ref, kcr_ref, vcr_ref, ks_ref, kw_ref, vst_ref, vwt_ref, bg_ref, gate_ref,
                pek_ref, w1k_ref, w2k_ref, pev_ref, w1v_ref, w2vt_ref,
                o_ref, ksa_ref, kwa_ref, vwp_ref, kca_ref, vct_ref, madd_ref, cpat_ref, score_ref, cin_ref, *,
                seq, gb):
    tq, kb_len, hpg, dh = NSA_TQ, NSA_KB, B_HPG, B_DH
    rows_all = hpg * tq
    n_slc = seq // SEL_LEN
    n_units = seq // CMP_STRIDE
    n_cmp = (seq - CMP_LEN) // CMP_STRIDE + 1
    qi = pl.program_id(2)
    t0 = pl.multiple_of(qi * tq, tq)

    @pl.when(qi == 0)
    def _():
        pos = lax.broadcasted_iota(jnp.int32, (seq, dh), 0)
        lane = lax.broadcasted_iota(jnp.int32, (seq, dh), 1)
        alibi = _position_features(pos, lane)
        blk_hot = jnp.where(lane == lax.shift_right_logical(pos, 6), 1, 0)
        chunk_hot = jnp.where(lane - LANE_CHUNK == lax.shift_right_logical(pos, 7), 1, 0)
        feat_s = (alibi + blk_hot + chunk_hot).astype(F32).astype(BF16)
        feat_w = alibi.astype(F32).astype(BF16)
        lane_w = lax.broadcasted_iota(jnp.int32, (WIN, dh), 1)
        pad_w = jnp.where(lane_w == LANE_PAD, 1.0, 0.0).astype(BF16)
        c_end = lax.broadcasted_iota(jnp.int32, (n_units, dh), 0) * CMP_STRIDE + (CMP_LEN - 1)
        lane_c = lax.broadcasted_iota(jnp.int32, (n_units, dh), 1)
        feat_c = _position_features(c_end, lane_c).astype(F32).astype(BF16)
        for gi in range(gb):
            gc = slice(gi * dh, (gi + 1) * dh)
            ksa_ref[gi, :, 0:dh] = ks_ref[:, gc].astype(BF16)
            ksa_ref[gi, :, dh:2 * dh] = feat_s
            kwa_ref[gi, WIN:WIN + seq, 0:dh] = kw_ref[:, gc].astype(BF16)
            kwa_ref[gi, WIN:WIN + seq, dh:2 * dh] = feat_w
            kwa_ref[gi, 0:WIN, 0:dh] = jnp.zeros((WIN, dh), BF16)
            kwa_ref[gi, 0:WIN, dh:2 * dh] = pad_w
            vwp_ref[gi, :, 0:WIN] = jnp.zeros((dh, WIN), BF16)
            vwp_ref[gi, :, WIN:WIN + seq] = vwt_ref[gc, :]
            cin_ref[...] = kcr_ref[:, gc]
            kca_ref[gi, :, 0:dh] = _compress_block(cin_ref, pek_ref, w1k_ref, w2k_ref, n_units).astype(BF16)
            kca_ref[gi, :, dh:2 * dh] = feat_c
            cin_ref[...] = vcr_ref[:, gc]
            vct_ref[gi] = _compress_block(cin_ref, pev_ref, w1v_ref, w2vt_ref, n_units,
                                          transposed=True).astype(BF16)
        j_loc = lax.broadcasted_iota(jnp.int32, (tq, rows_all), 0)
        t_loc = lax.broadcasted_iota(jnp.int32, (tq, rows_all), 1) & (tq - 1)
        madd_ref[0] = jnp.where(j_loc > t_loc, 0.0, -MASK_BIG)
        madd_ref[1] = jnp.where(j_loc <= t_loc, 0.0, -MASK_BIG)
        cpat_ref[...] = ((lax.broadcasted_iota(jnp.int32, (n_units, rows_all), 1) & (tq - 1))
                         - lax.broadcasted_iota(jnp.int32, (n_units, rows_all), 0) * CMP_STRIDE)

    def stack_heads(q2t, extras):
        return jnp.concatenate(
            [jnp.concatenate([q2t[h], extras[h].astype(BF16)], axis=0) for h in range(hpg)], axis=1)

    def softmax_cols(s):
        e = jnp.exp2(s - jnp.max(s, axis=0, keepdims=True))
        return e.astype(BF16), jnp.sum(e, axis=0, keepdims=True)

    row1 = lax.broadcasted_iota(jnp.int32, (dh, 1), 0)
    row_q = lax.broadcasted_iota(jnp.int32, (dh, tq), 0)
    span = WIN + tq
    gw = hpg * dh

    def before_loop(gi):
        qf = q_ref[:, gi * gw:(gi + 1) * gw]
        q2t = [jnp.transpose(qf[:, h * dh:(h + 1) * dh] * (dh ** -0.5 * LOG2E)).astype(BF16) for h in range(hpg)]
        feat = feat_ref[gi]
        q_w = stack_heads(q2t, [jnp.broadcast_to(feat[:, h:h + 1], (dh, tq)) for h in range(hpg)])

        s_w = jnp.dot(kwa_ref[gi, pl.ds(t0, span), :], q_w, preferred_element_type=F32)
        s_w = jnp.concatenate([s_w[0:tq] + madd_ref[0], s_w[tq:WIN], s_w[WIN:span] + madd_ref[1]], axis=0)
        e_w, l_w = softmax_cols(s_w)
        o_win = jnp.dot(vwp_ref[gi, :, pl.ds(t0, span)], e_w, preferred_element_type=F32) * (1.0 / l_w)

        mask_c = cpat_ref[...] >= (CMP_LEN - 1) - t0
        s_c = jnp.where(mask_c, jnp.dot(kca_ref[gi], q_w, preferred_element_type=F32), NEG_INF)
        e_c = jnp.where(mask_c, jnp.exp2(s_c - jnp.max(s_c, axis=0, keepdims=True)), 0.0)
        l_c = jnp.sum(e_c, axis=0, keepdims=True)
        p_c = e_c * (1.0 / jnp.where(l_c > 0.0, l_c, 1.0))
        o_cmp = jnp.dot(vct_ref[gi], p_c.astype(BF16), preferred_element_type=F32)

        p_sum = p_c[:, 0:tq]
        for h in range(1, hpg):
            p_sum = p_sum + p_c[:, h * tq:(h + 1) * tq]
        jn = lax.broadcasted_iota(jnp.int32, (n_slc, n_units), 0) * SEL_LEN
        cn = lax.broadcasted_iota(jnp.int32, (n_slc, n_units), 1) * CMP_STRIDE
        ov_t = jnp.where(cn <= jn + (SEL_LEN - 1),
                         jnp.where(cn + (CMP_LEN - 1) >= jn, jnp.where(cn < n_cmp * CMP_STRIDE, 1.0, 0.0), 0.0),
                         0.0).astype(BF16)
        p_hi = p_sum.astype(BF16)
        p_lo = (p_sum - p_hi.astype(F32)).astype(BF16)
        imp_t = (jnp.dot(ov_t, p_hi, preferred_element_type=F32)
                 + jnp.dot(ov_t, p_lo, preferred_element_type=F32))
        blk = lax.broadcasted_iota(jnp.int32, (n_slc, tq), 0)
        cur = lax.shift_right_logical(t0 + lax.broadcasted_iota(jnp.int32, (n_slc, tq), 1), 6)
        forced = jnp.where(blk == 0, 1, jnp.where(blk == cur, 1, jnp.where(blk == cur - 1, 1, 0)))
        score = jnp.where(forced > 0, FORCE_SCORE, jnp.where(blk <= cur, imp_t, NEG_INF))
        score_ref[gi] = score
        rank = jnp.zeros((n_slc, tq), jnp.int32)
        for j in range(n_slc):
            sj = score_ref[gi, pl.ds(j, 1), :]
            rank = rank + jnp.where(sj > score, 1, jnp.where(blk > j, jnp.where(sj == score, 1, 0), 0))
        sel_bias_t = jnp.where(rank < SEL_TOPK, 0.0, -MASK_BIG)
        sel_bias = jnp.concatenate([sel_bias_t, jnp.zeros((dh - n_slc, tq), F32)], axis=0)

        feat_np = jnp.where(row1 == LANE_PAD, 0.0, feat)
        chunk_col = jnp.where(row1 >= LANE_CHUNK,
                              jnp.where(row1 < LANE_CHUNK + seq // tq,
                                        jnp.where(row1 - LANE_CHUNK < qi, 0.0, -MASK_BIG), 0.0), 0.0)
        extra_d = [jnp.where(row_q < n_slc, sel_bias, feat_np[:, h:h + 1]) for h in range(hpg)]
        q_d = stack_heads(q2t, extra_d)
        q_s = stack_heads(q2t, [x + chunk_col for x in extra_d])
        s_d = jnp.dot(ksa_ref[gi, pl.ds(t0, tq), :], q_d, preferred_element_type=F32) + madd_ref[1]
        m0 = jnp.max(s_d, axis=0, keepdims=True)
        e_d = jnp.exp2(s_d - m0)
        l0 = jnp.sum(e_d, axis=0, keepdims=True)
        acc0 = jnp.dot(vst_ref[gi * dh:(gi + 1) * dh, pl.ds(t0, tq)], e_d.astype(BF16),
                       preferred_element_type=F32)
        return o_win, o_cmp, q_s, (m0, l0, acc0)

    pre = [before_loop(gi) for gi in range(gb)]

    def sel_step(kb, carry):
        k0 = pl.multiple_of(kb * kb_len, kb_len)
        out = []
        for gi in range(gb):
            m, l, acc = carry[gi]
            s = jnp.dot(ksa_ref[gi, pl.ds(k0, kb_len), :], pre[gi][2], preferred_element_type=F32)
            m_new = jnp.maximum(m, jnp.max(s, axis=0, keepdims=True))
            alpha = jnp.exp2(m - m_new)
            e = jnp.exp2(s - m_new)
            l = alpha * l + jnp.sum(e, axis=0, keepdims=True)
            acc = alpha * acc + jnp.dot(vst_ref[gi * dh:(gi + 1) * dh, pl.ds(k0, kb_len)], e.astype(BF16),
                                        preferred_element_type=F32)
            out.append((m_new, l, acc))
        return tuple(out)

    n_kb = lax.shift_right_logical(t0 + kb_len - 1, int(math.log2(kb_len)))
    sel = lax.fori_loop(0, n_kb, sel_step, tuple(p[3] for p in pre))

    for gi in range(gb):
        o_win, o_cmp = pre[gi][0], pre[gi][1]
        _, l_s, acc_s = sel[gi]
        o_sel = acc_s * (1.0 / l_s)
        sg_t = jnp.transpose(jax.nn.sigmoid(gate_ref[:, gi * dh:(gi + 1) * dh]))
        grow = lambda br: jnp.concatenate([sg_t[3 * h + br:3 * h + br + 1, :] for h in range(hpg)], axis=1)
        o = grow(0) * o_cmp + grow(1) * o_sel + grow(2) * o_win
        for h in range(hpg):
            c0 = gi * gw + h * dh
            o_h = jnp.transpose(o[:, h * tq:(h + 1) * tq])
            o_ref[:, c0:c0 + dh] = (o_h * _silu(bg_ref[:, c0:c0 + dh])).astype(o_ref.dtype)


def _slope_features():
    slopes = jnp.asarray(2.0 ** (-8.0 * np.arange(1, B_HEADS + 1) / B_HEADS) * LOG2E, dtype=F32)
    hi = slopes.astype(BF16).astype(F32)
    mid = (slopes - hi).astype(BF16).astype(F32)
    lo = (slopes - hi - mid).astype(BF16).astype(F32)
    cols = jnp.stack([64.0 * hi, hi, 64.0 * mid, mid, 64.0 * lo, lo, jnp.full_like(hi, -MASK_BIG)], axis=1)
    feat = jnp.zeros((B_HEADS, B_DH), F32).at[:, LANE_ALIBI:LANE_PAD + 1].set(cols)
    feat = feat.reshape(B_KV, B_HPG, B_DH)
    return jnp.swapaxes(jnp.pad(feat, ((0, 0), (0, 8 - B_HPG), (0, 0))), 1, 2)


def nsa(p, pt, pvt, pe_k, w1_k, w2_k, pe_v, w1_v, w2t_v, layer, bsz, seq,
        col_q, col_kc, col_vc, col_ks, col_kw, col_bg, col_gate):
    nq = seq // NSA_TQ
    gb = NSA_GB
    gw = B_HPG * B_DH
    n_units = seq // CMP_STRIDE
    rows_all = B_HPG * NSA_TQ
    assert all(c % gb == 0 for c in (col_q, col_kc, col_vc, col_ks, col_kw, col_bg, col_gate)) and B_KV % gb == 0
    tile = lambda c0, w: pl.BlockSpec((NSA_TQ, gb * w), lambda b, g, i: (b * nq + i, c0 // gb + g))
    full = lambda c0: pl.BlockSpec((seq, gb * B_DH), lambda b, g, i: (b, c0 // gb + g))
    full_t = lambda r0: pl.BlockSpec((gb * B_DH, seq), lambda b, g, i: (r0 // gb + g, b))
    wspec = lambda a: pl.BlockSpec((None,) + a.shape[1:], lambda b, g, i: (layer,) + (0,) * (a.ndim - 1))
    return pl.pallas_call(
        functools.partial(_nsa_kernel, seq=seq, gb=gb),
        grid=(bsz, B_KV // gb, nq),
        in_specs=[
            pl.BlockSpec((gb, B_DH, 8), lambda b, g, i: (g, 0, 0)),
            tile(col_q, gw), full(col_kc), full(col_vc), full(col_ks), full(col_kw),
            full_t(0), full_t(B_KV),
            tile(col_bg, gw), tile(col_gate, B_DH),
            wspec(pe_k), wspec(w1_k), wspec(w2_k), wspec(pe_v), wspec(w1_v), wspec(w2t_v),
        ],
        out_specs=pl.BlockSpec((NSA_TQ, gb * gw), lambda b, g, i: (b * nq + i, g)),
        out_shape=jax.ShapeDtypeStruct((bsz * seq, B_HEADS * B_DH), BF16),
        scratch_shapes=[
            pltpu.VMEM((gb, seq, 2 * B_DH), BF16),
            pltpu.VMEM((gb, seq + WIN, 2 * B_DH), BF16),
            pltpu.VMEM((gb, B_DH, seq + WIN), BF16),
            pltpu.VMEM((gb, n_units, 2 * B_DH), BF16),
            pltpu.VMEM((gb, B_DH, n_units), BF16),
            pltpu.VMEM((2, NSA_TQ, rows_all), F32),
            pltpu.VMEM((n_units, rows_all), jnp.int32),
            pltpu.VMEM((gb, seq // SEL_LEN, NSA_TQ), F32),
            pltpu.VMEM((seq, B_DH), F32),
        ],
        compiler_params=_params("parallel", "parallel", "arbitrary"),
        name="nsa_attention",
    )(_slope_features(), p, p, p, p, p, pvt, pvt, pt, pt, pe_k, w1_k, w2_k, pe_v, w1_v, w2t_v)


def _softplus(x):
    return jnp.maximum(x, 0.0) + jnp.log1p(jnp.exp(-jnp.abs(x)))


def _rglru_kernel(xb_ref, g_ref, cw_ref, cb_ref, wa_ref, ba_ref, wi_ref, bi_ref, lam_ref, o_ref,
                  xpad_ref, h_ref, ga_ref, gu_ref, gc_ref):
    tb = RG_TB
    pad = 8

    @pl.when(pl.program_id(2) == 0)
    def _():
        xpad_ref[0:pad, :] = jnp.zeros((pad, RG_BS), F32)
        h_ref[...] = jnp.zeros_like(h_ref)

    x = xb_ref[...]
    xpad_ref[pad:pad + tb, :] = x
    xc = xpad_ref[pl.ds(pad - 3, tb), :] * cw_ref[0:1, :]
    for j in range(1, CONV_W):
        xc = xc + xpad_ref[pl.ds(pad - 3 + j, tb), :] * cw_ref[j:j + 1, :]
    xc = xc + cb_ref[...]
    xpad_ref[0:pad, :] = x[tb - pad:tb]

    xc_b = xc.astype(BF16)
    r = jax.nn.sigmoid(jnp.dot(xc_b, wa_ref[...], preferred_element_type=F32) + ba_ref[...])
    i = jax.nn.sigmoid(jnp.dot(xc_b, wi_ref[...], preferred_element_type=F32) + bi_ref[...])
    log_a = -RG_C * _softplus(-lam_ref[...]) * r
    a = jnp.exp(log_a)
    th = jnp.tanh(log_a)
    u = jnp.sqrt(jnp.maximum(-2.0 * th / (1.0 - th), 0.0)) * (i * xc)

    def scan_rows(a, u, row, n, axis):
        sh = 1
        while sh < n:
            keep = row >= sh
            a_s = jnp.where(keep, pltpu.roll(a, sh, axis), 1.0)
            u_s = jnp.where(keep, pltpu.roll(u, sh, axis), 0.0)
            u = a * u_s + u
            a = a * a_s
            sh *= 2
        return a, u

    grp = 8
    ng = tb // grp
    a, u = scan_rows(a.reshape(ng, grp, RG_BS), u.reshape(ng, grp, RG_BS),
                     lax.broadcasted_iota(jnp.int32, (ng, grp, RG_BS), 1), grp, 1)
    a, u = a.reshape(tb, RG_BS), u.reshape(tb, RG_BS)
    row_g = lax.broadcasted_iota(jnp.int32, (ng, 128), 0)
    h_prev = h_ref[...]
    pieces = []
    for k in range(RG_BS // 128):
        lanes = slice(k * 128, (k + 1) * 128)
        ga_ref[k] = a[:, lanes]
        gu_ref[k] = u[:, lanes]
        a_g, u_g = scan_rows(ga_ref[k, pl.ds(grp - 1, ng, stride=grp), :],
                             gu_ref[k, pl.ds(grp - 1, ng, stride=grp), :], row_g, ng, 0)
        h_end = u_g + a_g * h_prev[:, lanes]
        h_ref[:, lanes] = h_end[ng - 1:ng]
        gc_ref[k] = jnp.where(row_g >= 1, pltpu.roll(h_end, 1, 0), h_prev[:, lanes])
        pieces.append(jnp.concatenate(
            [u[g * grp:(g + 1) * grp, lanes] + a[g * grp:(g + 1) * grp, lanes] * gc_ref[k, pl.ds(g, 1), :]
             for g in range(ng)], axis=0))
    h = jnp.concatenate(pieces, axis=1)
    o_ref[...] = (h * _silu(g_ref[...])).astype(o_ref.dtype)


def rglru(p, conv_w, conv_b, w_a, b_a, w_i, b_i, lam, layer, bsz, seq):
    nt = seq // RG_TB
    d_rnn = RG_BLOCKS * RG_BS
    tile = lambda c0: pl.BlockSpec((RG_TB, RG_BS), lambda b, n, t: (b * nt + t, c0 + n))
    vec = lambda rows: pl.BlockSpec((None, rows, RG_BS), lambda b, n, t: (layer, 0, n))
    mat = pl.BlockSpec((None, None, RG_BS, RG_BS), lambda b, n, t: (layer, n, 0, 0))
    n_layers = conv_b.shape[0]
    row = lambda a: a.reshape(n_layers, 1, d_rnn)
    return pl.pallas_call(
        _rglru_kernel,
        grid=(bsz, RG_BLOCKS, nt),
        in_specs=[tile(0), tile(RG_BLOCKS), vec(CONV_W), vec(1), mat, vec(1), mat, vec(1), vec(1)],
        out_specs=pl.BlockSpec((RG_TB, RG_BS), lambda b, n, t: (b * nt + t, n)),
        out_shape=jax.ShapeDtypeStruct((bsz * seq, d_rnn), BF16),
        scratch_shapes=[pltpu.VMEM((RG_TB + 8, RG_BS), F32), pltpu.VMEM((1, RG_BS), F32),
                        pltpu.VMEM((RG_BS // 128, RG_TB, 128), F32), pltpu.VMEM((RG_BS // 128, RG_TB, 128), F32),
                        pltpu.VMEM((RG_BS // 128, RG_TB // 8, 128), F32)],
        compiler_params=_params("parallel", "parallel", "arbitrary"),
        name="rglru",
    )(p, p, conv_w, row(conv_b), w_a, row(b_a), w_i, row(b_i), row(lam))


def _even_tail_weight(w, sizes):
    start = int(np.sum(sizes[:-2]))
    n_gate, d_b = sizes[-2], sizes[-1]
    gate = w[:, :, start:start + n_gate].reshape(w.shape[0], w.shape[1], B_KV, 3 * B_HPG)
    gate = jnp.pad(gate, ((0, 0), (0, 0), (0, 0), (0, B_DH - 3 * B_HPG))).reshape(w.shape[0], w.shape[1], B_KV * B_DH)
    return jnp.concatenate([w[:, :, start + n_gate:start + n_gate + d_b], gate], axis=2).astype(BF16)


def kernel(x, norm_w, final_norm_w, even_w_in, even_w_out, hgrn_lb_logits, hgrn_norm_w, cmp_pe_k, cmp_w1_k, cmp_w2_k, cmp_pe_v, cmp_w1_v, cmp_w2_v, odd_w_in, odd_w_out, rg_conv_w, rg_conv_b, rg_w_a, rg_b_a, rg_w_i, rg_b_i, rg_lambda):
    bsz, seq, d_model = x.shape
    depth = norm_w.shape[0]
    d_a = A_HEADS * A_DV
    d_b = B_HEADS * B_DH
    d_rnn = RG_BLOCKS * RG_BS
    a_qk = A_HEADS * A_DK
    kvw = B_KV * B_DH
    gw = B_HPG * B_DH
    assert seq % NSA_KB == 0 and seq % RG_TB == 0 and seq % HG_CHUNK == 0 and CMP_LEN == 2 * CMP_STRIDE
    assert seq // SEL_LEN <= LANE_ALIBI and seq // NSA_TQ <= B_DH - LANE_CHUNK and seq >= WIN + NSA_TQ
    sizes = (a_qk, a_qk, d_a, d_a, d_b, kvw, kvw, kvw, kvw, kvw, kvw, 3 * B_HEADS, d_b)
    names = ("a_q", "a_f", "a_i", "a_g", "b_q", "b_kc", "b_vc", "b_ks", "b_vs", "b_kw", "b_vw", "b_gate", "b_g")
    off = dict(zip(names, np.concatenate([[0], np.cumsum(sizes)[:-1]]).tolist()))
    tn = 512
    assert all(off[k] % tn == 0 for k in ("b_vs", "b_kw", "b_vw")) and kvw == tn
    skip_tile = off["b_vs"] // tn
    n_main_tiles = off["b_vw"] // tn - 1
    col_kw = (off["b_kw"] - tn) // B_DH

    even_in_b = even_w_in.astype(BF16)
    even_tail_b = _even_tail_weight(even_w_in, sizes)
    vcols = lambda name: jnp.swapaxes(even_w_in[:, :, off[name]:off[name] + kvw], 1, 2)
    even_vt_b = jnp.concatenate([vcols("b_vs"), vcols("b_vw")], axis=1).astype(BF16)
    even_out_b = even_w_out.astype(BF16)
    odd_in_b = odd_w_in.astype(BF16)
    odd_out_b = odd_w_out.astype(BF16)
    w_a_b, w_i_b = rg_w_a.astype(BF16), rg_w_i.astype(BF16)
    w1k_b, w2k_b = cmp_w1_k.astype(BF16), cmp_w2_k.astype(BF16)
    w1v_b, w2tv_b = cmp_w1_v.astype(BF16), jnp.swapaxes(cmp_w2_v, 1, 2).astype(BF16)

    xf = x.reshape(bsz * seq, d_model)
    for layer in range(depth):
        if layer % 2 == 0:
            e = layer // 2
            p = norm_matmul(xf, norm_w[layer], even_in_b, e, n_main_tiles, skip_tile)
            pt = norm_matmul(xf, norm_w[layer], even_tail_b, e, even_tail_b.shape[2] // tn)
            pvt = norm_matmul_t(xf, norm_w[layer], even_vt_b, e)
            ya = hgrn2(p, hgrn_lb_logits, hgrn_norm_w[e], e, bsz, seq,
                       off["a_q"] // A_DK, off["a_f"] // A_DK, off["a_i"] // A_DV, off["a_g"] // A_DV)
            yb = nsa(p, pt, pvt, cmp_pe_k, w1k_b, w2k_b, cmp_pe_v, w1v_b, w2tv_b, e, bsz, seq,
                     off["b_q"] // gw, off["b_kc"] // B_DH, off["b_vc"] // B_DH, off["b_ks"] // B_DH,
                     col_kw, 0, d_b // B_DH)
            xf = proj_residual([ya, yb], even_out_b, e, xf)
        else:
            o = layer // 2
            p = norm_matmul(xf, norm_w[layer], odd_in_b, o, 2 * d_rnn // tn)
            hr = rglru(p, rg_conv_w, rg_conv_b, w_a_b, rg_b_a, w_i_b, rg_b_i, rg_lambda, o, bsz, seq)
            if layer == depth - 1:
                xf = proj_residual([hr], odd_out_b, o, xf, tm=512, tn=d_model, norm_w=final_norm_w)
            else:
                xf = proj_residual([hr], odd_out_b, o, xf)
    if depth % 2 == 1:
        xf = rmsnorm(xf, final_norm_w)
    return xf.reshape(bsz, seq, d_model)
```

```python
import functools
import math

import jax
import jax.numpy as jnp
import numpy as np
from jax import lax
from jax.experimental import pallas as pl
from jax.experimental.pallas import tpu as pltpu

F32 = jnp.float32
BF16 = jnp.bfloat16

NORM_EPS = 1e-6
NEG_INF = -1e30
FORCE_SCORE = 1e30
TINY = 1e-30
LOG2E = 1.4426950408889634

A_HEADS = 16
A_DK = 128
A_DV = 128
HG_CHUNK = 64
HG_SUB = 8

B_HEADS = 16
B_DH = 128
B_KV = 4
B_HPG = B_HEADS // B_KV
CMP_LEN = 32
CMP_STRIDE = 16
SEL_LEN = 64
SEL_TOPK = 16
WIN = 512
NSA_TQ = 128
NSA_KB = 512
NSA_GB = 2
MASK_BIG = 2.0 ** 100

LANE_ALIBI = 32
LANE_PAD = 38
LANE_CHUNK = 40

RG_BLOCKS = 10
RG_BS = 256
CONV_W = 4
RG_C = 8.0
RG_TB = 256

VMEM_LIMIT = 56 * 1024 * 1024

_NT = (((1,), (1,)), ((), ()))
_TN = (((0,), (0,)), ((), ()))


def _silu(x):
    return x * jax.nn.sigmoid(x)


def _params(*sem):
    return pltpu.CompilerParams(dimension_semantics=sem, vmem_limit_bytes=VMEM_LIMIT)


def _norm_matmul_kernel(x_ref, nw_ref, w_ref, o_ref, h_ref):
    @pl.when(pl.program_id(1) == 0)
    def _():
        x = x_ref[...]
        ms = jnp.mean(x * x, axis=-1, keepdims=True)
        h_ref[...] = (x * lax.rsqrt(ms + NORM_EPS) * nw_ref[...]).astype(BF16)

    o_ref[...] = jnp.dot(h_ref[...], w_ref[...], preferred_element_type=F32)


def norm_matmul(x, nw, w, layer, n_tiles, skip_tile=None, tm=1024, tn=512):
    m, d = x.shape
    src = (lambda j: j) if skip_tile is None else (lambda j: j + j // skip_tile)
    assert skip_tile is None or n_tiles <= 2 * skip_tile
    return pl.pallas_call(
        _norm_matmul_kernel,
        grid=(m // tm, n_tiles),
        in_specs=[
            pl.BlockSpec((tm, d), lambda i, j: (i, 0)),
            pl.BlockSpec((1, d), lambda i, j: (0, 0)),
            pl.BlockSpec((None, d, tn), lambda i, j: (layer, 0, src(j))),
        ],
        out_specs=pl.BlockSpec((tm, tn), lambda i, j: (i, j)),
        out_shape=jax.ShapeDtypeStruct((m, n_tiles * tn), F32),
        scratch_shapes=[pltpu.VMEM((tm, d), BF16)],
        compiler_params=_params("parallel", "arbitrary"),
        name="norm_matmul",
    )(x, nw.reshape(1, d), w)


def _norm_matmul_t_kernel(x_ref, nw_ref, wt_ref, o_ref, h_ref):
    @pl.when(pl.program_id(1) == 0)
    def _():
        x = x_ref[...]
        ms = jnp.mean(x * x, axis=-1, keepdims=True)
        h_ref[...] = (x * lax.rsqrt(ms + NORM_EPS) * nw_ref[...]).astype(BF16)

    o_ref[...] = lax.dot_general(wt_ref[...], h_ref[...], _NT, preferred_element_type=F32).astype(o_ref.dtype)


def norm_matmul_t(x, nw, wt, layer, tm=1024, tn=512):
    m, d = x.shape
    n = wt.shape[1]
    return pl.pallas_call(
        _norm_matmul_t_kernel,
        grid=(m // tm, n // tn),
        in_specs=[
            pl.BlockSpec((tm, d), lambda i, j: (i, 0)),
            pl.BlockSpec((1, d), lambda i, j: (0, 0)),
            pl.BlockSpec((None, tn, d), lambda i, j: (layer, j, 0)),
        ],
        out_specs=pl.BlockSpec((tn, tm), lambda i, j: (j, i)),
        out_shape=jax.ShapeDtypeStruct((n, m), BF16),
        scratch_shapes=[pltpu.VMEM((tm, d), BF16)],
        compiler_params=_params("parallel", "arbitrary"),
        name="norm_matmul_t",
    )(x, nw.reshape(1, d), wt)


def _proj_residual_kernel(*refs, n_in, with_norm):
    a_refs, w_refs = refs[:n_in], refs[n_in:2 * n_in]
    x_ref = refs[2 * n_in]
    o_ref = refs[-1]
    y = jnp.dot(a_refs[0][...], w_refs[0][...], preferred_element_type=F32)
    for a_ref, w_ref in zip(a_refs[1:], w_refs[1:]):
        y = y + jnp.dot(a_ref[...], w_ref[...], preferred_element_type=F32)
    y = x_ref[...] + y
    if with_norm:
        y = y * lax.rsqrt(jnp.mean(y * y, axis=-1, keepdims=True) + NORM_EPS) * refs[2 * n_in + 1][...]
    o_ref[...] = y


def proj_residual(acts, w, layer, x, tm=1024, tn=512, norm_w=None):
    m, n = x.shape
    n_in = len(acts)
    k = acts[0].shape[1]
    assert norm_w is None or tn == n
    in_specs = [pl.BlockSpec((tm, k), lambda i, j: (i, 0)) for _ in acts]
    in_specs += [pl.BlockSpec((None, k, tn), functools.partial(lambda i, j, s: (layer, s, j), s=s))
                 for s in range(n_in)]
    in_specs += [pl.BlockSpec((tm, tn), lambda i, j: (i, j))]
    extra = []
    if norm_w is not None:
        in_specs += [pl.BlockSpec((1, n), lambda i, j: (0, 0))]
        extra = [norm_w.reshape(1, n)]
    return pl.pallas_call(
        functools.partial(_proj_residual_kernel, n_in=n_in, with_norm=norm_w is not None),
        grid=(m // tm, n // tn),
        in_specs=in_specs,
        out_specs=pl.BlockSpec((tm, tn), lambda i, j: (i, j)),
        out_shape=jax.ShapeDtypeStruct((m, n), F32),
        compiler_params=_params("parallel", "arbitrary"),
        name="proj_residual",
    )(*acts, *([w] * n_in), x, *extra)


def _rmsnorm_kernel(x_ref, w_ref, o_ref):
    x = x_ref[...]
    ms = jnp.mean(x * x, axis=-1, keepdims=True)
    o_ref[...] = x * lax.rsqrt(ms + NORM_EPS) * w_ref[...]


def rmsnorm(x, w, tm=512):
    m, d = x.shape
    return pl.pallas_call(
        _rmsnorm_kernel,
        grid=(m // tm,),
        in_specs=[pl.BlockSpec((tm, d), lambda i: (i, 0)), pl.BlockSpec((1, d), lambda i: (0, 0))],
        out_specs=pl.BlockSpec((tm, d), lambda i: (i, 0)),
        out_shape=jax.ShapeDtypeStruct((m, d), F32),
        compiler_params=_params("parallel"),
        name="final_rmsnorm",
    )(x, w.reshape(1, d))


def _split3(x):
    hi = x.astype(BF16)
    r1 = x - hi.astype(F32)
    mid = r1.astype(BF16)
    lo = (r1 - mid.astype(F32)).astype(BF16)
    return hi, mid, lo


def _hgrn2_levels(c_len, sub):
    levels, seg, half = [], 0, sub
    while half < c_len:
        levels.append((half, seg))
        seg += c_len // (2 * half)
        half *= 2
    return levels, seg


def _hgrn2_kernel(lbl_ref, gain_ref, q_ref, f_ref, v_ref, g_ref, o_ref, st_ref, b_ref, k_ref, vc_ref, *,
                  layer, hb, tblk):
    c_len, sub = HG_CHUNK, HG_SUB
    n_sub = c_len // sub
    levels, n_seg = _hgrn2_levels(c_len, sub)
    lg = lbl_ref[...]
    ex = jnp.exp(lg - jnp.max(lg, axis=0, keepdims=True))
    sm = ex / jnp.sum(ex, axis=0, keepdims=True)
    lb_all = sm[0:1]
    for j in range(1, layer + 1):
        lb_all = lb_all + sm[j:j + 1]
    lb_all = lb_all - sm[0:1]
    gain_all = gain_ref[...]

    @pl.when(pl.program_id(2) == 0)
    def _():
        st_ref[...] = jnp.zeros_like(st_ref)

    tri = jnp.where(lax.broadcasted_iota(jnp.int32, (c_len, c_len), 0)
                    >= lax.broadcasted_iota(jnp.int32, (c_len, c_len), 1), 1.0, 0.0).astype(BF16)
    ones_kk = jnp.ones((A_DK, A_DK), BF16)
    row_sub = lax.broadcasted_iota(jnp.int32, (sub, A_DK), 0)
    zero_blk = jnp.zeros((sub, A_DK), F32)

    def head_chunk(h, r0):
        rows = pl.ds(r0, c_len)
        cols = slice(h * A_DK, (h + 1) * A_DK)
        lb = lb_all[:, cols]
        one_m_lb = 1.0 - lb
        q = _silu(q_ref[rows, cols])
        fz = f_ref[rows, cols]
        v = v_ref[rows, cols]
        vc_ref[h] = v
        v_b = v.astype(BF16)
        f = lb + one_m_lb * jax.nn.sigmoid(fz)
        hi, mid, lo = _split3(jnp.log2(jnp.maximum(f, TINY)))
        kk = one_m_lb * jax.nn.sigmoid(-fz)
        b = (jnp.dot(tri, hi, preferred_element_type=F32)
             + jnp.dot(tri, mid, preferred_element_type=F32)
             + jnp.dot(tri, lo, preferred_element_type=F32))
        b_ref[h] = b
        k_ref[h] = kk
        st = st_ref[h]
        o_inter = lax.dot_general((q * jnp.exp2(b)).astype(BF16), st.astype(BF16), _NT,
                                  preferred_element_type=F32)

        es = []
        for j in range(n_sub):
            bj = b[j * sub:(j + 1) * sub]
            qj = q[j * sub:(j + 1) * sub]
            for s in range(sub):
                bs = b_ref[h, pl.ds(j * sub + s, 1), :]
                ks = k_ref[h, pl.ds(j * sub + s, 1), :]
                es.append(qj * jnp.exp2(jnp.where(row_sub >= s, bj - bs, NEG_INF)) * ks)
        att_d = jnp.dot(jnp.concatenate(es, axis=0).astype(BF16), ones_kk, preferred_element_type=F32)
        o_diag = []
        for j in range(n_sub):
            acc = None
            for s in range(sub):
                u = j * sub + s
                term = att_d[u * sub:(u + 1) * sub] * vc_ref[h, pl.ds(u, 1), :]
                acc = term if acc is None else acc + term
            o_diag.append(acc)

        q_rows, k_rows = [], []
        for j in range(n_sub):
            bj = b[j * sub:(j + 1) * sub]
            q_seg = [zero_blk] * n_seg
            k_seg = [zero_blk] * n_seg
            for half, seg0 in levels:
                blk = (j * sub) // (2 * half)
                m_row = b_ref[h, pl.ds(blk * 2 * half + half - 1, 1), :]
                if ((j * sub) // half) % 2 == 1:
                    q_seg[seg0 + blk] = q[j * sub:(j + 1) * sub] * jnp.exp2(bj - m_row)
                else:
                    k_seg[seg0 + blk] = kk[j * sub:(j + 1) * sub] * jnp.exp2(m_row - bj)
            q_rows.append(jnp.concatenate(q_seg, axis=1))
            k_rows.append(jnp.concatenate(k_seg, axis=1))
        att_o = lax.dot_general(jnp.concatenate(q_rows, axis=0).astype(BF16),
                                jnp.concatenate(k_rows, axis=0).astype(BF16), _NT,
                                preferred_element_type=F32)
        o = (o_inter + jnp.dot(att_o.astype(BF16), v_b, preferred_element_type=F32)
             + jnp.concatenate(o_diag, axis=0))

        b_last = b_ref[h, pl.ds(c_len - 1, 1), :]
        k_dec = (kk * jnp.exp2(b_last - b)).astype(BF16)
        st_ref[h] = st * jnp.exp2(b_last) + lax.dot_general(v_b, k_dec, _TN, preferred_element_type=F32)

        o = o * lax.rsqrt(jnp.mean(o * o, axis=-1, keepdims=True) + NORM_EPS) * gain_all[:, cols]
        o_ref[rows, cols] = (o * _silu(g_ref[rows, cols])).astype(o_ref.dtype)

    def chunk(c, carry):
        r0 = pl.multiple_of(c * c_len, c_len)
        for h in range(hb):
            head_chunk(h, r0)
        return carry

    lax.fori_loop(0, tblk // c_len, chunk, 0)


def hgrn2(p, lb_logits, gain, layer, bsz, seq, col_q, col_f, col_v, col_g, hb=16, tblk=256):
    n_layers = lb_logits.shape[0]
    nt = seq // tblk
    w = hb * A_DK
    blk = lambda c0: pl.BlockSpec((tblk, w), lambda b, h, t: (b * nt + t, c0 // hb + h))
    return pl.pallas_call(
        functools.partial(_hgrn2_kernel, layer=layer, hb=hb, tblk=tblk),
        grid=(bsz, A_HEADS // hb, nt),
        in_specs=[
            pl.BlockSpec((n_layers, w), lambda b, h, t: (0, h)),
            pl.BlockSpec((1, w), lambda b, h, t: (0, h)),
            blk(col_q), blk(col_f), blk(col_v), blk(col_g),
        ],
        out_specs=pl.BlockSpec((tblk, w), lambda b, h, t: (b * nt + t, h)),
        out_shape=jax.ShapeDtypeStruct((bsz * seq, A_HEADS * A_DV), BF16),
        scratch_shapes=[pltpu.VMEM((hb, A_DV, A_DK), F32), pltpu.VMEM((hb, HG_CHUNK, A_DK), F32),
                        pltpu.VMEM((hb, HG_CHUNK, A_DK), F32), pltpu.VMEM((hb, HG_CHUNK, A_DV), F32)],
        compiler_params=_params("parallel", "parallel", "arbitrary"),
        name="hgrn2",
    )(lb_logits, gain.reshape(1, -1), p, p, p, p)


def _position_features(pos, lane):
    ab = jnp.where((lane & 1) == 0, lax.shift_right_logical(pos, 6), pos & 63)
    return jnp.where(lane >= LANE_ALIBI, jnp.where(lane < LANE_PAD, ab, 0), 0)


def _compress_block(a_ref, pe_ref, w1_ref, w2_ref, n_units, transposed=False):
    h_a = jnp.zeros((n_units, B_DH), F32)
    h_b = jnp.zeros((n_units, B_DH), F32)
    for r in range(CMP_STRIDE):
        a_r = a_ref[pl.ds(r, n_units, stride=CMP_STRIDE), :]
        h_a = h_a + jnp.dot((a_r + pe_ref[r:r + 1, :]).astype(BF16), w1_ref[r * B_DH:(r + 1) * B_DH, :],
                            preferred_element_type=F32)
        r2 = CMP_STRIDE + r
        h_b = h_b + jnp.dot((a_r + pe_ref[r2:r2 + 1, :]).astype(BF16), w1_ref[r2 * B_DH:(r2 + 1) * B_DH, :],
                            preferred_element_type=F32)
    hid = _silu(h_a + pltpu.roll(h_b, n_units - 1, 0)).astype(BF16)
    if transposed:
        return lax.dot_general(w2_ref[...], hid, _NT, preferred_element_type=F32)
    return jnp.dot(hid, w2_ref[...], preferred_element_type=F32)


def _nsa_kernel(feat_ref, q_ref, kcr_ref, vcr_ref, ks_ref, kw_ref, vst_ref, vwt_ref, bg_ref, gate_ref,
                pek_ref, w1k_ref, w2k_ref, pev_ref, w1v_ref, w2vt_ref,
                o_ref, ksa_ref, kwa_ref, vwp_ref, kca_ref, vct_ref, madd_ref, cpat_ref, score_ref, cin_ref, *,
                seq, gb):
    tq, kb_len, hpg, dh = NSA_TQ, NSA_KB, B_HPG, B_DH
    rows_all = hpg * tq
    n_slc = seq // SEL_LEN
    n_units = seq // CMP_STRIDE
    n_cmp = (seq - CMP_LEN) // CMP_STRIDE + 1
    qi = pl.program_id(2)
    t0 = pl.multiple_of(qi * tq, tq)

    @pl.when(qi == 0)
    def _():
        pos = lax.broadcasted_iota(jnp.int32, (seq, dh), 0)
        lane = lax.broadcasted_iota(jnp.int32, (seq, dh), 1)
        alibi = _position_features(pos, lane)
        blk_hot = jnp.where(lane == lax.shift_right_logical(pos, 6), 1, 0)
        chunk_hot = jnp.where(lane - LANE_CHUNK == lax.shift_right_logical(pos, 7), 1, 0)
        feat_s = (alibi + blk_hot + chunk_hot).astype(F32).astype(BF16)
        feat_w = alibi.astype(F32).astype(BF16)
        lane_w = lax.broadcasted_iota(jnp.int32, (WIN, dh), 1)
        pad_w = jnp.where(lane_w == LANE_PAD, 1.0, 0.0).astype(BF16)
        c_end = lax.broadcasted_iota(jnp.int32, (n_units, dh), 0) * CMP_STRIDE + (CMP_LEN - 1)
        lane_c = lax.broadcasted_iota(jnp.int32, (n_units, dh), 1)
        feat_c = _position_features(c_end, lane_c).astype(F32).astype(BF16)
        for gi in range(gb):
            gc = slice(gi * dh, (gi + 1) * dh)
            ksa_ref[gi, :, 0:dh] = ks_ref[:, gc].astype(BF16)
            ksa_ref[gi, :, dh:2 * dh] = feat_s
            kwa_ref[gi, WIN:WIN + seq, 0:dh] = kw_ref[:, gc].astype(BF16)
            kwa_ref[gi, WIN:WIN + seq, dh:2 * dh] = feat_w
            kwa_ref[gi, 0:WIN, 0:dh] = jnp.zeros((WIN, dh), BF16)
            kwa_ref[gi, 0:WIN, dh:2 * dh] = pad_w
            vwp_ref[gi, :, 0:WIN] = jnp.zeros((dh, WIN), BF16)
            vwp_ref[gi, :, WIN:WIN + seq] = vwt_ref[gc, :]
            cin_ref[...] = kcr_ref[:, gc]
            kca_ref[gi, :, 0:dh] = _compress_block(cin_ref, pek_ref, w1k_ref, w2k_ref, n_units).astype(BF16)
            kca_ref[gi, :, dh:2 * dh] = feat_c
            cin_ref[...] = vcr_ref[:, gc]
            vct_ref[gi] = _compress_block(cin_ref, pev_ref, w1v_ref, w2vt_ref, n_units,
                                          transposed=True).astype(BF16)
        j_loc = lax.broadcasted_iota(jnp.int32, (tq, rows_all), 0)
        t_loc = lax.broadcasted_iota(jnp.int32, (tq, rows_all), 1) & (tq - 1)
        madd_ref[0] = jnp.where(j_loc > t_loc, 0.0, -MASK_BIG)
        madd_ref[1] = jnp.where(j_loc <= t_loc, 0.0, -MASK_BIG)
        cpat_ref[...] = ((lax.broadcasted_iota(jnp.int32, (n_units, rows_all), 1) & (tq - 1))
                         - lax.broadcasted_iota(jnp.int32, (n_units, rows_all), 0) * CMP_STRIDE)

    def stack_heads(q2t, extras):
        return jnp.concatenate(
            [jnp.concatenate([q2t[h], extras[h].astype(BF16)], axis=0) for h in range(hpg)], axis=1)

    def softmax_cols(s):
        e = jnp.exp2(s - jnp.max(s, axis=0, keepdims=True))
        return e.astype(BF16), jnp.sum(e, axis=0, keepdims=True)

    row1 = lax.broadcasted_iota(jnp.int32, (dh, 1), 0)
    row_q = lax.broadcasted_iota(jnp.int32, (dh, tq), 0)
    span = WIN + tq
    gw = hpg * dh

    def before_loop(gi):
        qf = q_ref[:, gi * gw:(gi + 1) * gw]
        q2t = [jnp.transpose(qf[:, h * dh:(h + 1) * dh] * (dh ** -0.5 * LOG2E)).astype(BF16) for h in range(hpg)]
        feat = feat_ref[gi]
        q_w = stack_heads(q2t, [jnp.broadcast_to(feat[:, h:h + 1], (dh, tq)) for h in range(hpg)])

        s_w = jnp.dot(kwa_ref[gi, pl.ds(t0, span), :], q_w, preferred_element_type=F32)
        s_w = jnp.concatenate([s_w[0:tq] + madd_ref[0], s_w[tq:WIN], s_w[WIN:span] + madd_ref[1]], axis=0)
        e_w, l_w = softmax_cols(s_w)
        o_win = jnp.dot(vwp_ref[gi, :, pl.ds(t0, span)], e_w, preferred_element_type=F32) * (1.0 / l_w)

        mask_c = cpat_ref[...] >= (CMP_LEN - 1) - t0
        s_c = jnp.where(mask_c, jnp.dot(kca_ref[gi], q_w, preferred_element_type=F32), NEG_INF)
        e_c = jnp.where(mask_c, jnp.exp2(s_c - jnp.max(s_c, axis=0, keepdims=True)), 0.0)
        l_c = jnp.sum(e_c, axis=0, keepdims=True)
        p_c = e_c * (1.0 / jnp.where(l_c > 0.0, l_c, 1.0))
        o_cmp = jnp.dot(vct_ref[gi], p_c.astype(BF16), preferred_element_type=F32)

        p_sum = p_c[:, 0:tq]
        for h in range(1, hpg):
            p_sum = p_sum + p_c[:, h * tq:(h + 1) * tq]
        jn = lax.broadcasted_iota(jnp.int32, (n_slc, n_units), 0) * SEL_LEN
        cn = lax.broadcasted_iota(jnp.int32, (n_slc, n_units), 1) * CMP_STRIDE
        ov_t = jnp.where(cn <= jn + (SEL_LEN - 1),
                         jnp.where(cn + (CMP_LEN - 1) >= jn, jnp.where(cn < n_cmp * CMP_STRIDE, 1.0, 0.0), 0.0),
                         0.0).astype(BF16)
        p_hi = p_sum.astype(BF16)
        p_lo = (p_sum - p_hi.astype(F32)).astype(BF16)
        imp_t = (jnp.dot(ov_t, p_hi, preferred_element_type=F32)
                 + jnp.dot(ov_t, p_lo, preferred_element_type=F32))
        blk = lax.broadcasted_iota(jnp.int32, (n_slc, tq), 0)
        cur = lax.shift_right_logical(t0 + lax.broadcasted_iota(jnp.int32, (n_slc, tq), 1), 6)
        forced = jnp.where(blk == 0, 1, jnp.where(blk == cur, 1, jnp.where(blk == cur - 1, 1, 0)))
        score = jnp.where(forced > 0, FORCE_SCORE, jnp.where(blk <= cur, imp_t, NEG_INF))
        score_ref[gi] = score
        rank = jnp.zeros((n_slc, tq), jnp.int32)
        for j in range(n_slc):
            sj = score_ref[gi, pl.ds(j, 1), :]
            rank = rank + jnp.where(sj > score, 1, jnp.where(blk > j, jnp.where(sj == score, 1, 0), 0))
        sel_bias_t = jnp.where(rank < SEL_TOPK, 0.0, -MASK_BIG)
        sel_bias = jnp.concatenate([sel_bias_t, jnp.zeros((dh - n_slc, tq), F32)], axis=0)

        feat_np = jnp.where(row1 == LANE_PAD, 0.0, feat)
        chunk_col = jnp.where(row1 >= LANE_CHUNK,
                              jnp.where(row1 < LANE_CHUNK + seq // tq,
                                        jnp.where(row1 - LANE_CHUNK < qi, 0.0, -MASK_BIG), 0.0), 0.0)
        extra_d = [jnp.where(row_q < n_slc, sel_bias, feat_np[:, h:h + 1]) for h in range(hpg)]
        q_d = stack_heads(q2t, extra_d)
        q_s = stack_heads(q2t, [x + chunk_col for x in extra_d])
        s_d = jnp.dot(ksa_ref[gi, pl.ds(t0, tq), :], q_d, preferred_element_type=F32) + madd_ref[1]
        m0 = jnp.max(s_d, axis=0, keepdims=True)
        e_d = jnp.exp2(s_d - m0)
        l0 = jnp.sum(e_d, axis=0, keepdims=True)
        acc0 = jnp.dot(vst_ref[gi * dh:(gi + 1) * dh, pl.ds(t0, tq)], e_d.astype(BF16),
                       preferred_element_type=F32)
        return o_win, o_cmp, q_s, (m0, l0, acc0)

    pre = [before_loop(gi) for gi in range(gb)]

    def sel_step(kb, carry):
        k0 = pl.multiple_of(kb * kb_len, kb_len)
        out = []
        for gi in range(gb):
            m, l, acc = carry[gi]
            s = jnp.dot(ksa_ref[gi, pl.ds(k0, kb_len), :], pre[gi][2], preferred_element_type=F32)
            m_new = jnp.maximum(m, jnp.max(s, axis=0, keepdims=True))
            alpha = jnp.exp2(m - m_new)
            e = jnp.exp2(s - m_new)
            l = alpha * l + jnp.sum(e, axis=0, keepdims=True)
            acc = alpha * acc + jnp.dot(vst_ref[gi * dh:(gi + 1) * dh, pl.ds(k0, kb_len)], e.astype(BF16),
                                        preferred_element_type=F32)
            out.append((m_new, l, acc))
        return tuple(out)

    n_kb = lax.shift_right_logical(t0 + kb_len - 1, int(math.log2(kb_len)))
    sel = lax.fori_loop(0, n_kb, sel_step, tuple(p[3] for p in pre))

    for gi in range(gb):
        o_win, o_cmp = pre[gi][0], pre[gi][1]
        _, l_s, acc_s = sel[gi]
        o_sel = acc_s * (1.0 / l_s)
        sg_t = jnp.transpose(jax.nn.sigmoid(gate_ref[:, gi * dh:(gi + 1) * dh]))
        grow = lambda br: jnp.concatenate([sg_t[3 * h + br:3 * h + br + 1, :] for h in range(hpg)], axis=1)
        o = grow(0) * o_cmp + grow(1) * o_sel + grow(2) * o_win
        for h in range(hpg):
            c0 = gi * gw + h * dh
            o_h = jnp.transpose(o[:, h * tq:(h + 1) * tq])
            o_ref[:, c0:c0 + dh] = (o_h * _silu(bg_ref[:, c0:c0 + dh])).astype(o_ref.dtype)


def _slope_features():
    slopes = jnp.asarray(2.0 ** (-8.0 * np.arange(1, B_HEADS + 1) / B_HEADS) * LOG2E, dtype=F32)
    hi = slopes.astype(BF16).astype(F32)
    mid = (slopes - hi).astype(BF16).astype(F32)
    lo = (slopes - hi - mid).astype(BF16).astype(F32)
    cols = jnp.stack([64.0 * hi, hi, 64.0 * mid, mid, 64.0 * lo, lo, jnp.full_like(hi, -MASK_BIG)], axis=1)
    feat = jnp.zeros((B_HEADS, B_DH), F32).at[:, LANE_ALIBI:LANE_PAD + 1].set(cols)
    feat = feat.reshape(B_KV, B_HPG, B_DH)
    return jnp.swapaxes(jnp.pad(feat, ((0, 0), (0, 8 - B_HPG), (0, 0))), 1, 2)


def nsa(p, pt, pvt, pe_k, w1_k, w2_k, pe_v, w1_v, w2t_v, layer, bsz, seq,
        col_q, col_kc, col_vc, col_ks, col_kw, col_bg, col_gate):
    nq = seq // NSA_TQ
    gb = NSA_GB
    gw = B_HPG * B_DH
    n_units = seq // CMP_STRIDE
    rows_all = B_HPG * NSA_TQ
    assert all(c % gb == 0 for c in (col_q, col_kc, col_vc, col_ks, col_kw, col_bg, col_gate)) and B_KV % gb == 0
    tile = lambda c0, w: pl.BlockSpec((NSA_TQ, gb * w), lambda b, g, i: (b * nq + i, c0 // gb + g))
    full = lambda c0: pl.BlockSpec((seq, gb * B_DH), lambda b, g, i: (b, c0 // gb + g))
    full_t = lambda r0: pl.BlockSpec((gb * B_DH, seq), lambda b, g, i: (r0 // gb + g, b))
    wspec = lambda a: pl.BlockSpec((None,) + a.shape[1:], lambda b, g, i: (layer,) + (0,) * (a.ndim - 1))
    return pl.pallas_call(
        functools.partial(_nsa_kernel, seq=seq, gb=gb),
        grid=(bsz, B_KV // gb, nq),
        in_specs=[
            pl.BlockSpec((gb, B_DH, 8), lambda b, g, i: (g, 0, 0)),
            tile(col_q, gw), full(col_kc), full(col_vc), full(col_ks), full(col_kw),
            full_t(0), full_t(B_KV),
            tile(col_bg, gw), tile(col_gate, B_DH),
            wspec(pe_k), wspec(w1_k), wspec(w2_k), wspec(pe_v), wspec(w1_v), wspec(w2t_v),
        ],
        out_specs=pl.BlockSpec((NSA_TQ, gb * gw), lambda b, g, i: (b * nq + i, g)),
        out_shape=jax.ShapeDtypeStruct((bsz * seq, B_HEADS * B_DH), BF16),
        scratch_shapes=[
            pltpu.VMEM((gb, seq, 2 * B_DH), BF16),
            pltpu.VMEM((gb, seq + WIN, 2 * B_DH), BF16),
            pltpu.VMEM((gb, B_DH, seq + WIN), BF16),
            pltpu.VMEM((gb, n_units, 2 * B_DH), BF16),
            pltpu.VMEM((gb, B_DH, n_units), BF16),
            pltpu.VMEM((2, NSA_TQ, rows_all), F32),
            pltpu.VMEM((n_units, rows_all), jnp.int32),
            pltpu.VMEM((gb, seq // SEL_LEN, NSA_TQ), F32),
            pltpu.VMEM((seq, B_DH), F32),
        ],
        compiler_params=_params("parallel", "parallel", "arbitrary"),
        name="nsa_attention",
    )(_slope_features(), p, p, p, p, p, pvt, pvt, pt, pt, pe_k, w1_k, w2_k, pe_v, w1_v, w2t_v)


def _softplus(x):
    return jnp.maximum(x, 0.0) + jnp.log1p(jnp.exp(-jnp.abs(x)))


def _rglru_kernel(xb_ref, g_ref, cw_ref, cb_ref, wa_ref, ba_ref, wi_ref, bi_ref, lam_ref, o_ref,
                  xpad_ref, h_ref, ga_ref, gu_ref, gc_ref):
    tb = RG_TB
    pad = 8

    @pl.when(pl.program_id(2) == 0)
    def _():
        xpad_ref[0:pad, :] = jnp.zeros((pad, RG_BS), F32)
        h_ref[...] = jnp.zeros_like(h_ref)

    x = xb_ref[...]
    xpad_ref[pad:pad + tb, :] = x
    xc = xpad_ref[pl.ds(pad - 3, tb), :] * cw_ref[0:1, :]
    for j in range(1, CONV_W):
        xc = xc + xpad_ref[pl.ds(pad - 3 + j, tb), :] * cw_ref[j:j + 1, :]
    xc = xc + cb_ref[...]
    xpad_ref[0:pad, :] = x[tb - pad:tb]

    xc_b = xc.astype(BF16)
    r = jax.nn.sigmoid(jnp.dot(xc_b, wa_ref[...], preferred_element_type=F32) + ba_ref[...])
    i = jax.nn.sigmoid(jnp.dot(xc_b, wi_ref[...], preferred_element_type=F32) + bi_ref[...])
    log_a = -RG_C * _softplus(-lam_ref[...]) * r
    a = jnp.exp(log_a)
    th = jnp.tanh(log_a)
    u = jnp.sqrt(jnp.maximum(-2.0 * th / (1.0 - th), 0.0)) * (i * xc)

    def scan_rows(a, u, row, n, axis):
        sh = 1
        while sh < n:
            keep = row >= sh
            a_s = jnp.where(keep, pltpu.roll(a, sh, axis), 1.0)
            u_s = jnp.where(keep, pltpu.roll(u, sh, axis), 0.0)
            u = a * u_s + u
            a = a * a_s
            sh *= 2
        return a, u

    grp = 8
    ng = tb // grp
    a, u = scan_rows(a.reshape(ng, grp, RG_BS), u.reshape(ng, grp, RG_BS),
                     lax.broadcasted_iota(jnp.int32, (ng, grp, RG_BS), 1), grp, 1)
    a, u = a.reshape(tb, RG_BS), u.reshape(tb, RG_BS)
    row_g = lax.broadcasted_iota(jnp.int32, (ng, 128), 0)
    h_prev = h_ref[...]
    pieces = []
    for k in range(RG_BS // 128):
        lanes = slice(k * 128, (k + 1) * 128)
        ga_ref[k] = a[:, lanes]
        gu_ref[k] = u[:, lanes]
        a_g, u_g = scan_rows(ga_ref[k, pl.ds(grp - 1, ng, stride=grp), :],
                             gu_ref[k, pl.ds(grp - 1, ng, stride=grp), :], row_g, ng, 0)
        h_end = u_g + a_g * h_prev[:, lanes]
        h_ref[:, lanes] = h_end[ng - 1:ng]
        gc_ref[k] = jnp.where(row_g >= 1, pltpu.roll(h_end, 1, 0), h_prev[:, lanes])
        pieces.append(jnp.concatenate(
            [u[g * grp:(g + 1) * grp, lanes] + a[g * grp:(g + 1) * grp, lanes] * gc_ref[k, pl.ds(g, 1), :]
             for g in range(ng)], axis=0))
    h = jnp.concatenate(pieces, axis=1)
    o_ref[...] = (h * _silu(g_ref[...])).astype(o_ref.dtype)


def rglru(p, conv_w, conv_b, w_a, b_a, w_i, b_i, lam, layer, bsz, seq):
    nt = seq // RG_TB
    d_rnn = RG_BLOCKS * RG_BS
    tile = lambda c0: pl.BlockSpec((RG_TB, RG_BS), lambda b, n, t: (b * nt + t, c0 + n))
    vec = lambda rows: pl.BlockSpec((None, rows, RG_BS), lambda b, n, t: (layer, 0, n))
    mat = pl.BlockSpec((None, None, RG_BS, RG_BS), lambda b, n, t: (layer, n, 0, 0))
    n_layers = conv_b.shape[0]
    row = lambda a: a.reshape(n_layers, 1, d_rnn)
    return pl.pallas_call(
        _rglru_kernel,
        grid=(bsz, RG_BLOCKS, nt),
        in_specs=[tile(0), tile(RG_BLOCKS), vec(CONV_W), vec(1), mat, vec(1), mat, vec(1), vec(1)],
        out_specs=pl.BlockSpec((RG_TB, RG_BS), lambda b, n, t: (b * nt + t, n)),
        out_shape=jax.ShapeDtypeStruct((bsz * seq, d_rnn), BF16),
        scratch_shapes=[pltpu.VMEM((RG_TB + 8, RG_BS), F32), pltpu.VMEM((1, RG_BS), F32),
                        pltpu.VMEM((RG_BS // 128, RG_TB, 128), F32), pltpu.VMEM((RG_BS // 128, RG_TB, 128), F32),
                        pltpu.VMEM((RG_BS // 128, RG_TB // 8, 128), F32)],
        compiler_params=_params("parallel", "parallel", "arbitrary"),
        name="rglru",
    )(p, p, conv_w, row(conv_b), w_a, row(b_a), w_i, row(b_i), row(lam))


def _even_tail_weight(w, sizes):
    start = int(np.sum(sizes[:-2]))
    n_gate, d_b = sizes[-2], sizes[-1]
    gate = w[:, :, start:start + n_gate].reshape(w.shape[0], w.shape[1], B_KV, 3 * B_HPG)
    gate = jnp.pad(gate, ((0, 0), (0, 0), (0, 0), (0, B_DH - 3 * B_HPG))).reshape(w.shape[0], w.shape[1], B_KV * B_DH)
    return jnp.concatenate([w[:, :, start + n_gate:start + n_gate + d_b], gate], axis=2).astype(BF16)


def kernel(x, norm_w, final_norm_w, even_w_in, even_w_out, hgrn_lb_logits, hgrn_norm_w, cmp_pe_k, cmp_w1_k, cmp_w2_k, cmp_pe_v, cmp_w1_v, cmp_w2_v, odd_w_in, odd_w_out, rg_conv_w, rg_conv_b, rg_w_a, rg_b_a, rg_w_i, rg_b_i, rg_lambda):
    bsz, seq, d_model = x.shape
    depth = norm_w.shape[0]
    d_a = A_HEADS * A_DV
    d_b = B_HEADS * B_DH
    d_rnn = RG_BLOCKS * RG_BS
    a_qk = A_HEADS * A_DK
    kvw = B_KV * B_DH
    gw = B_HPG * B_DH
    assert seq % NSA_KB == 0 and seq % RG_TB == 0 and seq % HG_CHUNK == 0 and CMP_LEN == 2 * CMP_STRIDE
    assert seq // SEL_LEN <= LANE_ALIBI and seq // NSA_TQ <= B_DH - LANE_CHUNK and seq >= WIN + NSA_TQ
    sizes = (a_qk, a_qk, d_a, d_a, d_b, kvw, kvw, kvw, kvw, kvw, kvw, 3 * B_HEADS, d_b)
    names = ("a_q", "a_f", "a_i", "a_g", "b_q", "b_kc", "b_vc", "b_ks", "b_vs", "b_kw", "b_vw", "b_gate", "b_g")
    off = dict(zip(names, np.concatenate([[0], np.cumsum(sizes)[:-1]]).tolist()))
    tn = 512
    assert all(off[k] % tn == 0 for k in ("b_vs", "b_kw", "b_vw")) and kvw == tn
    skip_tile = off["b_vs"] // tn
    n_main_tiles = off["b_vw"] // tn - 1
    col_kw = (off["b_kw"] - tn) // B_DH

    even_in_b = even_w_in.astype(BF16)
    even_tail_b = _even_tail_weight(even_w_in, sizes)
    vcols = lambda name: jnp.swapaxes(even_w_in[:, :, off[name]:off[name] + kvw], 1, 2)
    even_vt_b = jnp.concatenate([vcols("b_vs"), vcols("b_vw")], axis=1).astype(BF16)
    even_out_b = even_w_out.astype(BF16)
    odd_in_b = odd_w_in.astype(BF16)
    odd_out_b = odd_w_out.astype(BF16)
    w_a_b, w_i_b = rg_w_a.astype(BF16), rg_w_i.astype(BF16)
    w1k_b, w2k_b = cmp_w1_k.astype(BF16), cmp_w2_k.astype(BF16)
    w1v_b, w2tv_b = cmp_w1_v.astype(BF16), jnp.swapaxes(cmp_w2_v, 1, 2).astype(BF16)

    xf = x.reshape(bsz * seq, d_model)
    for layer in range(depth):
        if layer % 2 == 0:
            e = layer // 2
            p = norm_matmul(xf, norm_w[layer], even_in_b, e, n_main_tiles, skip_tile)
            pt = norm_matmul(xf, norm_w[layer], even_tail_b, e, even_tail_b.shape[2] // tn)
            pvt = norm_matmul_t(xf, norm_w[layer], even_vt_b, e)
            ya = hgrn2(p, hgrn_lb_logits, hgrn_norm_w[e], e, bsz, seq,
                       off["a_q"] // A_DK, off["a_f"] // A_DK, off["a_i"] // A_DV, off["a_g"] // A_DV)
            yb = nsa(p, pt, pvt, cmp_pe_k, w1k_b, w2k_b, cmp_pe_v, w1v_b, w2tv_b, e, bsz, seq,
                     off["b_q"] // gw, off["b_kc"] // B_DH, off["b_vc"] // B_DH, off["b_ks"] // B_DH,
                     col_kw, 0, d_b // B_DH)
            xf = proj_residual([ya, yb], even_out_b, e, xf)
        else:
            o = layer // 2
            p = norm_matmul(xf, norm_w[layer], odd_in_b, o, 2 * d_rnn // tn)
            hr = rglru(p, rg_conv_w, rg_conv_b, w_a_b, rg_b_a, w_i_b, rg_b_i, rg_lambda, o, bsz, seq)
            if layer == depth - 1:
                xf = proj_residual([hr], odd_out_b, o, xf, tm=512, tn=d_model, norm_w=final_norm_w)
            else:
                xf = proj_residual([hr], odd_out_b, o, xf)
    if depth % 2 == 1:
        xf = rmsnorm(xf, final_norm_w)
    return xf.reshape(bsz, seq, d_model)
```

```python
import functools
import math

import jax
import jax.numpy as jnp
import numpy as np
from jax import lax
from jax.experimental import pallas as pl
from jax.experimental.pallas import tpu as pltpu

F32 = jnp.float32
BF16 = jnp.bfloat16

NORM_EPS = 1e-6
NEG_INF = -1e30
FORCE_SCORE = 1e30
TINY = 1e-30
LOG2E = 1.4426950408889634

A_HEADS = 16
A_DK = 128
A_DV = 128
HG_CHUNK = 64
HG_SUB = 8

B_HEADS = 16
B_DH = 128
B_KV = 4
B_HPG = B_HEADS // B_KV
CMP_LEN = 32
CMP_STRIDE = 16
SEL_LEN = 64
SEL_TOPK = 16
WIN = 512
NSA_TQ = 128
NSA_KB = 512
NSA_GB = 2
MASK_BIG = 2.0 ** 100

LANE_ALIBI = 32
LANE_PAD = 38
LANE_CHUNK = 40

RG_BLOCKS = 10
RG_BS = 256
CONV_W = 4
RG_C = 8.0
RG_TB = 512

VMEM_LIMIT = 56 * 1024 * 1024

_NT = (((1,), (1,)), ((), ()))
_TN = (((0,), (0,)), ((), ()))


def _silu(x):
    return x * jax.nn.sigmoid(x)


def _params(*sem):
    return pltpu.CompilerParams(dimension_semantics=sem, vmem_limit_bytes=VMEM_LIMIT)


def _norm_matmul_kernel(x_ref, nw_ref, w_ref, o_ref, h_ref):
    @pl.when(pl.program_id(1) == 0)
    def _():
        x = x_ref[...]
        ms = jnp.mean(x * x, axis=-1, keepdims=True)
        h_ref[...] = (x * lax.rsqrt(ms + NORM_EPS) * nw_ref[...]).astype(BF16)

    o_ref[...] = jnp.dot(h_ref[...], w_ref[...], preferred_element_type=F32)


def norm_matmul(x, nw, w, layer, n_tiles, skip_tile=None, tm=1024, tn=512):
    m, d = x.shape
    src = (lambda j: j) if skip_tile is None else (lambda j: j + j // skip_tile)
    assert skip_tile is None or n_tiles <= 2 * skip_tile
    return pl.pallas_call(
        _norm_matmul_kernel,
        grid=(m // tm, n_tiles),
        in_specs=[
            pl.BlockSpec((tm, d), lambda i, j: (i, 0)),
            pl.BlockSpec((1, d), lambda i, j: (0, 0)),
            pl.BlockSpec((None, d, tn), lambda i, j: (layer, 0, src(j))),
        ],
        out_specs=pl.BlockSpec((tm, tn), lambda i, j: (i, j)),
        out_shape=jax.ShapeDtypeStruct((m, n_tiles * tn), F32),
        scratch_shapes=[pltpu.VMEM((tm, d), BF16)],
        compiler_params=_params("parallel", "arbitrary"),
        name="norm_matmul",
    )(x, nw.reshape(1, d), w)


def _norm_matmul_t_kernel(x_ref, nw_ref, wt_ref, o_ref, h_ref):
    @pl.when(pl.program_id(1) == 0)
    def _():
        x = x_ref[...]
        ms = jnp.mean(x * x, axis=-1, keepdims=True)
        h_ref[...] = (x * lax.rsqrt(ms + NORM_EPS) * nw_ref[...]).astype(BF16)

    o_ref[...] = lax.dot_general(wt_ref[...], h_ref[...], _NT, preferred_element_type=F32).astype(o_ref.dtype)


def norm_matmul_t(x, nw, wt, layer, tm=1024, tn=512):
    m, d = x.shape
    n = wt.shape[1]
    return pl.pallas_call(
        _norm_matmul_t_kernel,
        grid=(m // tm, n // tn),
        in_specs=[
            pl.BlockSpec((tm, d), lambda i, j: (i, 0)),
            pl.BlockSpec((1, d), lambda i, j: (0, 0)),
            pl.BlockSpec((None, tn, d), lambda i, j: (layer, j, 0)),
        ],
        out_specs=pl.BlockSpec((tn, tm), lambda i, j: (j, i)),
        out_shape=jax.ShapeDtypeStruct((n, m), BF16),
        scratch_shapes=[pltpu.VMEM((tm, d), BF16)],
        compiler_params=_params("parallel", "arbitrary"),
        name="norm_matmul_t",
    )(x, nw.reshape(1, d), wt)


def _proj_residual_kernel(*refs, n_in, with_norm):
    a_refs, w_refs = refs[:n_in], refs[n_in:2 * n_in]
    x_ref = refs[2 * n_in]
    o_ref = refs[-1]
    y = jnp.dot(a_refs[0][...], w_refs[0][...], preferred_element_type=F32)
    for a_ref, w_ref in zip(a_refs[1:], w_refs[1:]):
        y = y + jnp.dot(a_ref[...], w_ref[...], preferred_element_type=F32)
    y = x_ref[...] + y
    if with_norm:
        y = y * lax.rsqrt(jnp.mean(y * y, axis=-1, keepdims=True) + NORM_EPS) * refs[2 * n_in + 1][...]
    o_ref[...] = y


def proj_residual(acts, w, layer, x, tm=1024, tn=512, norm_w=None):
    m, n = x.shape
    n_in = len(acts)
    k = acts[0].shape[1]
    assert norm_w is None or tn == n
    in_specs = [pl.BlockSpec((tm, k), lambda i, j: (i, 0)) for _ in acts]
    in_specs += [pl.BlockSpec((None, k, tn), functools.partial(lambda i, j, s: (layer, s, j), s=s))
                 for s in range(n_in)]
    in_specs += [pl.BlockSpec((tm, tn), lambda i, j: (i, j))]
    extra = []
    if norm_w is not None:
        in_specs += [pl.BlockSpec((1, n), lambda i, j: (0, 0))]
        extra = [norm_w.reshape(1, n)]
    return pl.pallas_call(
        functools.partial(_proj_residual_kernel, n_in=n_in, with_norm=norm_w is not None),
        grid=(m // tm, n // tn),
        in_specs=in_specs,
        out_specs=pl.BlockSpec((tm, tn), lambda i, j: (i, j)),
        out_shape=jax.ShapeDtypeStruct((m, n), F32),
        compiler_params=_params("parallel", "arbitrary"),
        name="proj_residual",
    )(*acts, *([w] * n_in), x, *extra)


def _rmsnorm_kernel(x_ref, w_ref, o_ref):
    x = x_ref[...]
    ms = jnp.mean(x * x, axis=-1, keepdims=True)
    o_ref[...] = x * lax.rsqrt(ms + NORM_EPS) * w_ref[...]


def rmsnorm(x, w, tm=512):
    m, d = x.shape
    return pl.pallas_call(
        _rmsnorm_kernel,
        grid=(m // tm,),
        in_specs=[pl.BlockSpec((tm, d), lambda i: (i, 0)), pl.BlockSpec((1, d), lambda i: (0, 0))],
        out_specs=pl.BlockSpec((tm, d), lambda i: (i, 0)),
        out_shape=jax.ShapeDtypeStruct((m, d), F32),
        compiler_params=_params("parallel"),
        name="final_rmsnorm",
    )(x, w.reshape(1, d))


def _split3(x):
    hi = x.astype(BF16)
    r1 = x - hi.astype(F32)
    mid = r1.astype(BF16)
    lo = (r1 - mid.astype(F32)).astype(BF16)
    return hi, mid, lo


def _hgrn2_levels(c_len, sub):
    levels, seg, half = [], 0, sub
    while half < c_len:
        levels.append((half, seg))
        seg += c_len // (2 * half)
        half *= 2
    return levels, seg


def _hgrn2_kernel(lbl_ref, gain_ref, q_ref, f_ref, v_ref, g_ref, o_ref, st_ref, b_ref, k_ref, vc_ref, *,
                  layer, hb, tblk):
    c_len, sub = HG_CHUNK, HG_SUB
    n_sub = c_len // sub
    levels, n_seg = _hgrn2_levels(c_len, sub)
    lg = lbl_ref[...]
    ex = jnp.exp(lg - jnp.max(lg, axis=0, keepdims=True))
    sm = ex / jnp.sum(ex, axis=0, keepdims=True)
    lb_all = sm[0:1]
    for j in range(1, layer + 1):
        lb_all = lb_all + sm[j:j + 1]
    lb_all = lb_all - sm[0:1]
    gain_all = gain_ref[...]

    @pl.when(pl.program_id(2) == 0)
    def _():
        st_ref[...] = jnp.zeros_like(st_ref)

    tri = jnp.where(lax.broadcasted_iota(jnp.int32, (c_len, c_len), 0)
                    >= lax.broadcasted_iota(jnp.int32, (c_len, c_len), 1), 1.0, 0.0).astype(BF16)
    ones_kk = jnp.ones((A_DK, A_DK), BF16)
    row_sub = lax.broadcasted_iota(jnp.int32, (sub, A_DK), 0)
    zero_blk = jnp.zeros((sub, A_DK), F32)

    def head_chunk(h, r0):
        rows = pl.ds(r0, c_len)
        cols = slice(h * A_DK, (h + 1) * A_DK)
        lb = lb_all[:, cols]
        one_m_lb = 1.0 - lb
        q = _silu(q_ref[rows, cols])
        fz = f_ref[rows, cols]
        v = v_ref[rows, cols]
        vc_ref[h] = v
        v_b = v.astype(BF16)
        f = lb + one_m_lb * jax.nn.sigmoid(fz)
        hi, mid, lo = _split3(jnp.log2(jnp.maximum(f, TINY)))
        kk = one_m_lb * jax.nn.sigmoid(-fz)
        b = (jnp.dot(tri, hi, preferred_element_type=F32)
             + jnp.dot(tri, mid, preferred_element_type=F32)
             + jnp.dot(tri, lo, preferred_element_type=F32))
        b_ref[h] = b
        k_ref[h] = kk
        st = st_ref[h]
        o_inter = lax.dot_general((q * jnp.exp2(b)).astype(BF16), st.astype(BF16), _NT,
                                  preferred_element_type=F32)

        es = []
        for j in range(n_sub):
            bj = b[j * sub:(j + 1) * sub]
            qj = q[j * sub:(j + 1) * sub]
            for s in range(sub):
                bs = b_ref[h, pl.ds(j * sub + s, 1), :]
                ks = k_ref[h, pl.ds(j * sub + s, 1), :]
                es.append(qj * jnp.exp2(jnp.where(row_sub >= s, bj - bs, NEG_INF)) * ks)
        att_d = jnp.dot(jnp.concatenate(es, axis=0).astype(BF16), ones_kk, preferred_element_type=F32)
        o_diag = []
        for j in range(n_sub):
            acc = None
            for s in range(sub):
                u = j * sub + s
                term = att_d[u * sub:(u + 1) * sub] * vc_ref[h, pl.ds(u, 1), :]
                acc = term if acc is None else acc + term
            o_diag.append(acc)

        q_rows, k_rows = [], []
        for j in range(n_sub):
            bj = b[j * sub:(j + 1) * sub]
            q_seg = [zero_blk] * n_seg
            k_seg = [zero_blk] * n_seg
            for half, seg0 in levels:
                blk = (j * sub) // (2 * half)
                m_row = b_ref[h, pl.ds(blk * 2 * half + half - 1, 1), :]
                if ((j * sub) // half) % 2 == 1:
                    q_seg[seg0 + blk] = q[j * sub:(j + 1) * sub] * jnp.exp2(bj - m_row)
                else:
                    k_seg[seg0 + blk] = kk[j * sub:(j + 1) * sub] * jnp.exp2(m_row - bj)
            q_rows.append(jnp.concatenate(q_seg, axis=1))
            k_rows.append(jnp.concatenate(k_seg, axis=1))
        att_o = lax.dot_general(jnp.concatenate(q_rows, axis=0).astype(BF16),
                                jnp.concatenate(k_rows, axis=0).astype(BF16), _NT,
                                preferred_element_type=F32)
        o = (o_inter + jnp.dot(att_o.astype(BF16), v_b, preferred_element_type=F32)
             + jnp.concatenate(o_diag, axis=0))

        b_last = b_ref[h, pl.ds(c_len - 1, 1), :]
        k_dec = (kk * jnp.exp2(b_last - b)).astype(BF16)
        st_ref[h] = st * jnp.exp2(b_last) + lax.dot_general(v_b, k_dec, _TN, preferred_element_type=F32)

        o = o * lax.rsqrt(jnp.mean(o * o, axis=-1, keepdims=True) + NORM_EPS) * gain_all[:, cols]
        o_ref[rows, cols] = (o * _silu(g_ref[rows, cols])).astype(o_ref.dtype)

    def chunk(c, carry):
        r0 = pl.multiple_of(c * c_len, c_len)
        for h in range(hb):
            head_chunk(h, r0)
        return carry

    lax.fori_loop(0, tblk // c_len, chunk, 0)


def hgrn2(p, lb_logits, gain, layer, bsz, seq, col_q, col_f, col_v, col_g, hb=16, tblk=128):
    n_layers = lb_logits.shape[0]
    nt = seq // tblk
    w = hb * A_DK
    blk = lambda c0: pl.BlockSpec((tblk, w), lambda b, h, t: (b * nt + t, c0 // hb + h))
    return pl.pallas_call(
        functools.partial(_hgrn2_kernel, layer=layer, hb=hb, tblk=tblk),
        grid=(bsz, A_HEADS // hb, nt),
        in_specs=[
            pl.BlockSpec((n_layers, w), lambda b, h, t: (0, h)),
            pl.BlockSpec((1, w), lambda b, h, t: (0, h)),
            blk(col_q), blk(col_f), blk(col_v), blk(col_g),
        ],
        out_specs=pl.BlockSpec((tblk, w), lambda b, h, t: (b * nt + t, h)),
        out_shape=jax.ShapeDtypeStruct((bsz * seq, A_HEADS * A_DV), BF16),
        scratch_shapes=[pltpu.VMEM((hb, A_DV, A_DK), F32), pltpu.VMEM((hb, HG_CHUNK, A_DK), F32),
                        pltpu.VMEM((hb, HG_CHUNK, A_DK), F32), pltpu.VMEM((hb, HG_CHUNK, A_DV), F32)],
        compiler_params=_params("parallel", "parallel", "arbitrary"),
        name="hgrn2",
    )(lb_logits, gain.reshape(1, -1), p, p, p, p)


def _position_features(pos, lane):
    ab = jnp.where((lane & 1) == 0, lax.shift_right_logical(pos, 6), pos & 63)
    return jnp.where(lane >= LANE_ALIBI, jnp.where(lane < LANE_PAD, ab, 0), 0)


def _compress_block(a_ref, pe_ref, w1_ref, w2_ref, n_units, transposed=False):
    h_a = jnp.zeros((n_units, B_DH), F32)
    h_b = jnp.zeros((n_units, B_DH), F32)
    for r in range(CMP_STRIDE):
        a_r = a_ref[pl.ds(r, n_units, stride=CMP_STRIDE), :]
        h_a = h_a + jnp.dot((a_r + pe_ref[r:r + 1, :]).astype(BF16), w1_ref[r * B_DH:(r + 1) * B_DH, :],
                            preferred_element_type=F32)
        r2 = CMP_STRIDE + r
        h_b = h_b + jnp.dot((a_r + pe_ref[r2:r2 + 1, :]).astype(BF16), w1_ref[r2 * B_DH:(r2 + 1) * B_DH, :],
                            preferred_element_type=F32)
    hid = _silu(h_a + pltpu.roll(h_b, n_units - 1, 0)).astype(BF16)
    if transposed:
        return lax.dot_general(w2_ref[...], hid, _NT, preferred_element_type=F32)
    return jnp.dot(hid, w2_ref[...], preferred_element_type=F32)


def _nsa_kernel(feat_ref, q_ref, kcr_ref, vcr_ref, ks_ref, kw_ref, vst_ref, vwt_ref, bg_ref, gate_ref,
                pek_ref, w1k_ref, w2k_ref, pev_ref, w1v_ref, w2vt_ref,
                o_ref, ksa_ref, kwa_ref, vwp_ref, kca_ref, vct_ref, madd_ref, cpat_ref, score_ref, cin_ref, *,
                seq, gb):
    tq, kb_len, hpg, dh = NSA_TQ, NSA_KB, B_HPG, B_DH
    rows_all = hpg * tq
    n_slc = seq // SEL_LEN
    n_units = seq // CMP_STRIDE
    n_cmp = (seq - CMP_LEN) // CMP_STRIDE + 1
    qi = pl.program_id(2)
    t0 = pl.multiple_of(qi * tq, tq)

    @pl.when(qi == 0)
    def _():
        pos = lax.broadcasted_iota(jnp.int32, (seq, dh), 0)
        lane = lax.broadcasted_iota(jnp.int32, (seq, dh), 1)
        alibi = _position_features(pos, lane)
        blk_hot = jnp.where(lane == lax.shift_right_logical(pos, 6), 1, 0)
        chunk_hot = jnp.where(lane - LANE_CHUNK == lax.shift_right_logical(pos, 7), 1, 0)
        feat_s = (alibi + blk_hot + chunk_hot).astype(F32).astype(BF16)
        feat_w = alibi.astype(F32).astype(BF16)
        lane_w = lax.broadcasted_iota(jnp.int32, (WIN, dh), 1)
        pad_w = jnp.where(lane_w == LANE_PAD, 1.0, 0.0).astype(BF16)
        c_end = lax.broadcasted_iota(jnp.int32, (n_units, dh), 0) * CMP_STRIDE + (CMP_LEN - 1)
        lane_c = lax.broadcasted_iota(jnp.int32, (n_units, dh), 1)
        feat_c = _position_features(c_end, lane_c).astype(F32).astype(BF16)
        for gi in range(gb):
            gc = slice(gi * dh, (gi + 1) * dh)
            ksa_ref[gi, :, 0:dh] = ks_ref[:, gc].astype(BF16)
            ksa_ref[gi, :, dh:2 * dh] = feat_s
            kwa_ref[gi, WIN:WIN + seq, 0:dh] = kw_ref[:, gc].astype(BF16)
            kwa_ref[gi, WIN:WIN + seq, dh:2 * dh] = feat_w
            kwa_ref[gi, 0:WIN, 0:dh] = jnp.zeros((WIN, dh), BF16)
            kwa_ref[gi, 0:WIN, dh:2 * dh] = pad_w
            vwp_ref[gi, :, 0:WIN] = jnp.zeros((dh, WIN), BF16)
            vwp_ref[gi, :, WIN:WIN + seq] = vwt_ref[gc, :]
            cin_ref[...] = kcr_ref[:, gc]
            kca_ref[gi, :, 0:dh] = _compress_block(cin_ref, pek_ref, w1k_ref, w2k_ref, n_units).astype(BF16)
            kca_ref[gi, :, dh:2 * dh] = feat_c
            cin_ref[...] = vcr_ref[:, gc]
            vct_ref[gi] = _compress_block(cin_ref, pev_ref, w1v_ref, w2vt_ref, n_units,
                                          transposed=True).astype(BF16)
        j_loc = lax.broadcasted_iota(jnp.int32, (tq, rows_all), 0)
        t_loc = lax.broadcasted_iota(jnp.int32, (tq, rows_all), 1) & (tq - 1)
        madd_ref[0] = jnp.where(j_loc > t_loc, 0.0, -MASK_BIG)
        madd_ref[1] = jnp.where(j_loc <= t_loc, 0.0, -MASK_BIG)
        cpat_ref[...] = ((lax.broadcasted_iota(jnp.int32, (n_units, rows_all), 1) & (tq - 1))
                         - lax.broadcasted_iota(jnp.int32, (n_units, rows_all), 0) * CMP_STRIDE)

    def stack_heads(q2t, extras):
        return jnp.concatenate(
            [jnp.concatenate([q2t[h], extras[h].astype(BF16)], axis=0) for h in range(hpg)], axis=1)

    def softmax_cols(s):
        e = jnp.exp2(s - jnp.max(s, axis=0, keepdims=True))
        return e.astype(BF16), jnp.sum(e, axis=0, keepdims=True)

    row1 = lax.broadcasted_iota(jnp.int32, (dh, 1), 0)
    row_q = lax.broadcasted_iota(jnp.int32, (dh, tq), 0)
    span = WIN + tq
    gw = hpg * dh

    def before_loop(gi):
        qf = q_ref[:, gi * gw:(gi + 1) * gw]
        q2t = [jnp.transpose(qf[:, h * dh:(h + 1) * dh] * (dh ** -0.5 * LOG2E)).astype(BF16) for h in range(hpg)]
        feat = feat_ref[gi]
        q_w = stack_heads(q2t, [jnp.broadcast_to(feat[:, h:h + 1], (dh, tq)) for h in range(hpg)])

        s_w = jnp.dot(kwa_ref[gi, pl.ds(t0, span), :], q_w, preferred_element_type=F32)
        s_w = jnp.concatenate([s_w[0:tq] + madd_ref[0], s_w[tq:WIN], s_w[WIN:span] + madd_ref[1]], axis=0)
        e_w, l_w = softmax_cols(s_w)
        o_win = jnp.dot(vwp_ref[gi, :, pl.ds(t0, span)], e_w, preferred_element_type=F32) * (1.0 / l_w)

        mask_c = cpat_ref[...] >= (CMP_LEN - 1) - t0
        s_c = jnp.where(mask_c, jnp.dot(kca_ref[gi], q_w, preferred_element_type=F32), NEG_INF)
        e_c = jnp.where(mask_c, jnp.exp2(s_c - jnp.max(s_c, axis=0, keepdims=True)), 0.0)
        l_c = jnp.sum(e_c, axis=0, keepdims=True)
        p_c = e_c * (1.0 / jnp.where(l_c > 0.0, l_c, 1.0))
        o_cmp = jnp.dot(vct_ref[gi], p_c.astype(BF16), preferred_element_type=F32)

        p_sum = p_c[:, 0:tq]
        for h in range(1, hpg):
            p_sum = p_sum + p_c[:, h * tq:(h + 1) * tq]
        jn = lax.broadcasted_iota(jnp.int32, (n_slc, n_units), 0) * SEL_LEN
        cn = lax.broadcasted_iota(jnp.int32, (n_slc, n_units), 1) * CMP_STRIDE
        ov_t = jnp.where(cn <= jn + (SEL_LEN - 1),
                         jnp.where(cn + (CMP_LEN - 1) >= jn, jnp.where(cn < n_cmp * CMP_STRIDE, 1.0, 0.0), 0.0),
                         0.0).astype(BF16)
        p_hi = p_sum.astype(BF16)
        p_lo = (p_sum - p_hi.astype(F32)).astype(BF16)
        imp_t = (jnp.dot(ov_t, p_hi, preferred_element_type=F32)
                 + jnp.dot(ov_t, p_lo, preferred_element_type=F32))
        blk = lax.broadcasted_iota(jnp.int32, (n_slc, tq), 0)
        cur = lax.shift_right_logical(t0 + lax.broadcasted_iota(jnp.int32, (n_slc, tq), 1), 6)
        forced = jnp.where(blk == 0, 1, jnp.where(blk == cur, 1, jnp.where(blk == cur - 1, 1, 0)))
        score = jnp.where(forced > 0, FORCE_SCORE, jnp.where(blk <= cur, imp_t, NEG_INF))
        score_ref[gi] = score
        rank = jnp.zeros((n_slc, tq), jnp.int32)
        for j in range(n_slc):
            sj = score_ref[gi, pl.ds(j, 1), :]
            rank = rank + jnp.where(sj > score, 1, jnp.where(blk > j, jnp.where(sj == score, 1, 0), 0))
        sel_bias_t = jnp.where(rank < SEL_TOPK, 0.0, -MASK_BIG)
        sel_bias = jnp.concatenate([sel_bias_t, jnp.zeros((dh - n_slc, tq), F32)], axis=0)

        feat_np = jnp.where(row1 == LANE_PAD, 0.0, feat)
        chunk_col = jnp.where(row1 >= LANE_CHUNK,
                              jnp.where(row1 < LANE_CHUNK + seq // tq,
                                        jnp.where(row1 - LANE_CHUNK < qi, 0.0, -MASK_BIG), 0.0), 0.0)
        extra_d = [jnp.where(row_q < n_slc, sel_bias, feat_np[:, h:h + 1]) for h in range(hpg)]
        q_d = stack_heads(q2t, extra_d)
        q_s = stack_heads(q2t, [x + chunk_col for x in extra_d])
        s_d = jnp.dot(ksa_ref[gi, pl.ds(t0, tq), :], q_d, preferred_element_type=F32) + madd_ref[1]
        m0 = jnp.max(s_d, axis=0, keepdims=True)
        e_d = jnp.exp2(s_d - m0)
        l0 = jnp.sum(e_d, axis=0, keepdims=True)
        acc0 = jnp.dot(vst_ref[gi * dh:(gi + 1) * dh, pl.ds(t0, tq)], e_d.astype(BF16),
                       preferred_element_type=F32)
        return o_win, o_cmp, q_s, (m0, l0, acc0)

    pre = [before_loop(gi) for gi in range(gb)]

    def sel_step(kb, carry):
        k0 = pl.multiple_of(kb * kb_len, kb_len)
        out = []
        for gi in range(gb):
            m, l, acc = carry[gi]
            s = jnp.dot(ksa_ref[gi, pl.ds(k0, kb_len), :], pre[gi][2], preferred_element_type=F32)
            m_new = jnp.maximum(m, jnp.max(s, axis=0, keepdims=True))
            alpha = jnp.exp2(m - m_new)
            e = jnp.exp2(s - m_new)
            l = alpha * l + jnp.sum(e, axis=0, keepdims=True)
            acc = alpha * acc + jnp.dot(vst_ref[gi * dh:(gi + 1) * dh, pl.ds(k0, kb_len)], e.astype(BF16),
                                        preferred_element_type=F32)
            out.append((m_new, l, acc))
        return tuple(out)

    n_kb = lax.shift_right_logical(t0 + kb_len - 1, int(math.log2(kb_len)))
    sel = lax.fori_loop(0, n_kb, sel_step, tuple(p[3] for p in pre))

    for gi in range(gb):
        o_win, o_cmp = pre[gi][0], pre[gi][1]
        _, l_s, acc_s = sel[gi]
        o_sel = acc_s * (1.0 / l_s)
        sg_t = jnp.transpose(jax.nn.sigmoid(gate_ref[:, gi * dh:(gi + 1) * dh]))
        grow = lambda br: jnp.concatenate([sg_t[3 * h + br:3 * h + br + 1, :] for h in range(hpg)], axis=1)
        o = grow(0) * o_cmp + grow(1) * o_sel + grow(2) * o_win
        for h in range(hpg):
            c0 = gi * gw + h * dh
            o_h = jnp.transpose(o[:, h * tq:(h + 1) * tq])
            o_ref[:, c0:c0 + dh] = (o_h * _silu(bg_ref[:, c0:c0 + dh])).astype(o_ref.dtype)


def _slope_features():
    slopes = jnp.asarray(2.0 ** (-8.0 * np.arange(1, B_HEADS + 1) / B_HEADS) * LOG2E, dtype=F32)
    hi = slopes.astype(BF16).astype(F32)
    mid = (slopes - hi).astype(BF16).astype(F32)
    lo = (slopes - hi - mid).astype(BF16).astype(F32)
    cols = jnp.stack([64.0 * hi, hi, 64.0 * mid, mid, 64.0 * lo, lo, jnp.full_like(hi, -MASK_BIG)], axis=1)
    feat = jnp.zeros((B_HEADS, B_DH), F32).at[:, LANE_ALIBI:LANE_PAD + 1].set(cols)
    feat = feat.reshape(B_KV, B_HPG, B_DH)
    return jnp.swapaxes(jnp.pad(feat, ((0, 0), (0, 8 - B_HPG), (0, 0))), 1, 2)


def nsa(p, pt, pvt, pe_k, w1_k, w2_k, pe_v, w1_v, w2t_v, layer, bsz, seq,
        col_q, col_kc, col_vc, col_ks, col_kw, col_bg, col_gate):
    nq = seq // NSA_TQ
    gb = NSA_GB
    gw = B_HPG * B_DH
    n_units = seq // CMP_STRIDE
    rows_all = B_HPG * NSA_TQ
    assert all(c % gb == 0 for c in (col_q, col_kc, col_vc, col_ks, col_kw, col_bg, col_gate)) and B_KV % gb == 0
    tile = lambda c0, w: pl.BlockSpec((NSA_TQ, gb * w), lambda b, g, i: (b * nq + i, c0 // gb + g))
    full = lambda c0: pl.BlockSpec((seq, gb * B_DH), lambda b, g, i: (b, c0 // gb + g))
    full_t = lambda r0: pl.BlockSpec((gb * B_DH, seq), lambda b, g, i: (r0 // gb + g, b))
    wspec = lambda a: pl.BlockSpec((None,) + a.shape[1:], lambda b, g, i: (layer,) + (0,) * (a.ndim - 1))
    return pl.pallas_call(
        functools.partial(_nsa_kernel, seq=seq, gb=gb),
        grid=(bsz, B_KV // gb, nq),
        in_specs=[
            pl.BlockSpec((gb, B_DH, 8), lambda b, g, i: (g, 0, 0)),
            tile(col_q, gw), full(col_kc), full(col_vc), full(col_ks), full(col_kw),
            full_t(0), full_t(B_KV),
            tile(col_bg, gw), tile(col_gate, B_DH),
            wspec(pe_k), wspec(w1_k), wspec(w2_k), wspec(pe_v), wspec(w1_v), wspec(w2t_v),
        ],
        out_specs=pl.BlockSpec((NSA_TQ, gb * gw), lambda b, g, i: (b * nq + i, g)),
        out_shape=jax.ShapeDtypeStruct((bsz * seq, B_HEADS * B_DH), BF16),
        scratch_shapes=[
            pltpu.VMEM((gb, seq, 2 * B_DH), BF16),
            pltpu.VMEM((gb, seq + WIN, 2 * B_DH), BF16),
            pltpu.VMEM((gb, B_DH, seq + WIN), BF16),
            pltpu.VMEM((gb, n_units, 2 * B_DH), BF16),
            pltpu.VMEM((gb, B_DH, n_units), BF16),
            pltpu.VMEM((2, NSA_TQ, rows_all), F32),
            pltpu.VMEM((n_units, rows_all), jnp.int32),
            pltpu.VMEM((gb, seq // SEL_LEN, NSA_TQ), F32),
            pltpu.VMEM((seq, B_DH), F32),
        ],
        compiler_params=_params("parallel", "parallel", "arbitrary"),
        name="nsa_attention",
    )(_slope_features(), p, p, p, p, p, pvt, pvt, pt, pt, pe_k, w1_k, w2_k, pe_v, w1_v, w2t_v)


def _softplus(x):
    return jnp.maximum(x, 0.0) + jnp.log1p(jnp.exp(-jnp.abs(x)))


def _rglru_kernel(xb_ref, g_ref, cw_ref, cb_ref, wa_ref, ba_ref, wi_ref, bi_ref, lam_ref, o_ref,
                  xpad_ref, h_ref, ga_ref, gu_ref, gc_ref):
    tb = RG_TB
    pad = 8

    @pl.when(pl.program_id(2) == 0)
    def _():
        xpad_ref[0:pad, :] = jnp.zeros((pad, RG_BS), F32)
        h_ref[...] = jnp.zeros_like(h_ref)

    x = xb_ref[...]
    xpad_ref[pad:pad + tb, :] = x
    xc = xpad_ref[pl.ds(pad - 3, tb), :] * cw_ref[0:1, :]
    for j in range(1, CONV_W):
        xc = xc + xpad_ref[pl.ds(pad - 3 + j, tb), :] * cw_ref[j:j + 1, :]
    xc = xc + cb_ref[...]
    xpad_ref[0:pad, :] = x[tb - pad:tb]

    xc_b = xc.astype(BF16)
    r = jax.nn.sigmoid(jnp.dot(xc_b, wa_ref[...], preferred_element_type=F32) + ba_ref[...])
    i = jax.nn.sigmoid(jnp.dot(xc_b, wi_ref[...], preferred_element_type=F32) + bi_ref[...])
    log_a = -RG_C * _softplus(-lam_ref[...]) * r
    a = jnp.exp(log_a)
    th = jnp.tanh(log_a)
    u = jnp.sqrt(jnp.maximum(-2.0 * th / (1.0 - th), 0.0)) * (i * xc)

    def scan_rows(a, u, row, n, axis):
        sh = 1
        while sh < n:
            keep = row >= sh
            a_s = jnp.where(keep, pltpu.roll(a, sh, axis), 1.0)
            u_s = jnp.where(keep, pltpu.roll(u, sh, axis), 0.0)
            u = a * u_s + u
            a = a * a_s
            sh *= 2
        return a, u

    grp = 8
    ng = tb // grp
    a, u = scan_rows(a.reshape(ng, grp, RG_BS), u.reshape(ng, grp, RG_BS),
                     lax.broadcasted_iota(jnp.int32, (ng, grp, RG_BS), 1), grp, 1)
    a, u = a.reshape(tb, RG_BS), u.reshape(tb, RG_BS)
    row_g = lax.broadcasted_iota(jnp.int32, (ng, 128), 0)
    h_prev = h_ref[...]
    pieces = []
    for k in range(RG_BS // 128):
        lanes = slice(k * 128, (k + 1) * 128)
        ga_ref[k] = a[:, lanes]
        gu_ref[k] = u[:, lanes]
        a_g, u_g = scan_rows(ga_ref[k, pl.ds(grp - 1, ng, stride=grp), :],
                             gu_ref[k, pl.ds(grp - 1, ng, stride=grp), :], row_g, ng, 0)
        h_end = u_g + a_g * h_prev[:, lanes]
        h_ref[:, lanes] = h_end[ng - 1:ng]
        gc_ref[k] = jnp.where(row_g >= 1, pltpu.roll(h_end, 1, 0), h_prev[:, lanes])
        pieces.append(jnp.concatenate(
            [u[g * grp:(g + 1) * grp, lanes] + a[g * grp:(g + 1) * grp, lanes] * gc_ref[k, pl.ds(g, 1), :]
             for g in range(ng)], axis=0))
    h = jnp.concatenate(pieces, axis=1)
    o_ref[...] = (h * _silu(g_ref[...])).astype(o_ref.dtype)


def rglru(p, conv_w, conv_b, w_a, b_a, w_i, b_i, lam, layer, bsz, seq):
    nt = seq // RG_TB
    d_rnn = RG_BLOCKS * RG_BS
    tile = lambda c0: pl.BlockSpec((RG_TB, RG_BS), lambda b, n, t: (b * nt + t, c0 + n))
    vec = lambda rows: pl.BlockSpec((None, rows, RG_BS), lambda b, n, t: (layer, 0, n))
    mat = pl.BlockSpec((None, None, RG_BS, RG_BS), lambda b, n, t: (layer, n, 0, 0))
    n_layers = conv_b.shape[0]
    row = lambda a: a.reshape(n_layers, 1, d_rnn)
    return pl.pallas_call(
        _rglru_kernel,
        grid=(bsz, RG_BLOCKS, nt),
        in_specs=[tile(0), tile(RG_BLOCKS), vec(CONV_W), vec(1), mat, vec(1), mat, vec(1), vec(1)],
        out_specs=pl.BlockSpec((RG_TB, RG_BS), lambda b, n, t: (b * nt + t, n)),
        out_shape=jax.ShapeDtypeStruct((bsz * seq, d_rnn), BF16),
        scratch_shapes=[pltpu.VMEM((RG_TB + 8, RG_BS), F32), pltpu.VMEM((1, RG_BS), F32),
                        pltpu.VMEM((RG_BS // 128, RG_TB, 128), F32), pltpu.VMEM((RG_BS // 128, RG_TB, 128), F32),
                        pltpu.VMEM((RG_BS // 128, RG_TB // 8, 128), F32)],
        compiler_params=_params("parallel", "parallel", "arbitrary"),
        name="rglru",
    )(p, p, conv_w, row(conv_b), w_a, row(b_a), w_i, row(b_i), row(lam))


def _even_tail_weight(w, sizes):
    start = int(np.sum(sizes[:-2]))
    n_gate, d_b = sizes[-2], sizes[-1]
    gate = w[:, :, start:start + n_gate].reshape(w.shape[0], w.shape[1], B_KV, 3 * B_HPG)
    gate = jnp.pad(gate, ((0, 0), (0, 0), (0, 0), (0, B_DH - 3 * B_HPG))).reshape(w.shape[0], w.shape[1], B_KV * B_DH)
    return jnp.concatenate([w[:, :, start + n_gate:start + n_gate + d_b], gate], axis=2).astype(BF16)


def kernel(x, norm_w, final_norm_w, even_w_in, even_w_out, hgrn_lb_logits, hgrn_norm_w, cmp_pe_k, cmp_w1_k, cmp_w2_k, cmp_pe_v, cmp_w1_v, cmp_w2_v, odd_w_in, odd_w_out, rg_conv_w, rg_conv_b, rg_w_a, rg_b_a, rg_w_i, rg_b_i, rg_lambda):
    bsz, seq, d_model = x.shape
    depth = norm_w.shape[0]
    d_a = A_HEADS * A_DV
    d_b = B_HEADS * B_DH
    d_rnn = RG_BLOCKS * RG_BS
    a_qk = A_HEADS * A_DK
    kvw = B_KV * B_DH
    gw = B_HPG * B_DH
    assert seq % NSA_KB == 0 and seq % RG_TB == 0 and seq % HG_CHUNK == 0 and CMP_LEN == 2 * CMP_STRIDE
    assert seq // SEL_LEN <= LANE_ALIBI and seq // NSA_TQ <= B_DH - LANE_CHUNK and seq >= WIN + NSA_TQ
    sizes = (a_qk, a_qk, d_a, d_a, d_b, kvw, kvw, kvw, kvw, kvw, kvw, 3 * B_HEADS, d_b)
    names = ("a_q", "a_f", "a_i", "a_g", "b_q", "b_kc", "b_vc", "b_ks", "b_vs", "b_kw", "b_vw", "b_gate", "b_g")
    off = dict(zip(names, np.concatenate([[0], np.cumsum(sizes)[:-1]]).tolist()))
    tn = 512
    assert all(off[k] % tn == 0 for k in ("b_vs", "b_kw", "b_vw")) and kvw == tn
    skip_tile = off["b_vs"] // tn
    n_main_tiles = off["b_vw"] // tn - 1
    col_kw = (off["b_kw"] - tn) // B_DH

    even_in_b = even_w_in.astype(BF16)
    even_tail_b = _even_tail_weight(even_w_in, sizes)
    vcols = lambda name: jnp.swapaxes(even_w_in[:, :, off[name]:off[name] + kvw], 1, 2)
    even_vt_b = jnp.concatenate([vcols("b_vs"), vcols("b_vw")], axis=1).astype(BF16)
    even_out_b = even_w_out.astype(BF16)
    odd_in_b = odd_w_in.astype(BF16)
    odd_out_b = odd_w_out.astype(BF16)
    w_a_b, w_i_b = rg_w_a.astype(BF16), rg_w_i.astype(BF16)
    w1k_b, w2k_b = cmp_w1_k.astype(BF16), cmp_w2_k.astype(BF16)
    w1v_b, w2tv_b = cmp_w1_v.astype(BF16), jnp.swapaxes(cmp_w2_v, 1, 2).astype(BF16)

    xf = x.reshape(bsz * seq, d_model)
    for layer in range(depth):
        if layer % 2 == 0:
            e = layer // 2
            p = norm_matmul(xf, norm_w[layer], even_in_b, e, n_main_tiles, skip_tile)
            pt = norm_matmul(xf, norm_w[layer], even_tail_b, e, even_tail_b.shape[2] // tn)
            pvt = norm_matmul_t(xf, norm_w[layer], even_vt_b, e)
            ya = hgrn2(p, hgrn_lb_logits, hgrn_norm_w[e], e, bsz, seq,
                       off["a_q"] // A_DK, off["a_f"] // A_DK, off["a_i"] // A_DV, off["a_g"] // A_DV)
            yb = nsa(p, pt, pvt, cmp_pe_k, w1k_b, w2k_b, cmp_pe_v, w1v_b, w2tv_b, e, bsz, seq,
                     off["b_q"] // gw, off["b_kc"] // B_DH, off["b_vc"] // B_DH, off["b_ks"] // B_DH,
                     col_kw, 0, d_b // B_DH)
            xf = proj_residual([ya, yb], even_out_b, e, xf)
        else:
            o = layer // 2
            p = norm_matmul(xf, norm_w[layer], odd_in_b, o, 2 * d_rnn // tn)
            hr = rglru(p, rg_conv_w, rg_conv_b, w_a_b, rg_b_a, w_i_b, rg_b_i, rg_lambda, o, bsz, seq)
            if layer == depth - 1:
                xf = proj_residual([hr], odd_out_b, o, xf, tm=512, tn=d_model, norm_w=final_norm_w)
            else:
                xf = proj_residual([hr], odd_out_b, o, xf)
    if depth % 2 == 1:
        xf = rmsnorm(xf, final_norm_w)
    return xf.reshape(bsz, seq, d_model)
```
